```python
import math
import jax
import jax.numpy as jnp
from jax import lax
import numpy as np


D_MODEL = 2048
BATCH = 8
SEQ = 4096
DEPTH = 4

CTX_LEN = 256
GRID_W = 64
MIX_W = D_MODEL
CONV_W = MIX_W // 4
SSM_W = MIX_W // 4
ATTN_W = MIX_W - CONV_W - SSM_W
CONV_K = 3
SSM_GROUP = 16
SSM_GROUPS = SSM_W // SSM_GROUP
SSM_STATE = 64
HEAD_DIM = 128
N_HEADS = ATTN_W // HEAD_DIM
N_KV_HEADS = 2
GQA_GROUP = N_HEADS // N_KV_HEADS
KV_W = N_KV_HEADS * HEAD_DIM
Q_BLOCK = 128
ROPE_THETA = 10000.0
RMS_EPS = 1e-6
FFN_HIDDEN = -(-8 * D_MODEL // (3 * 256)) * 256
N_MOD = 6

CONV_V_OFF = 0
CONV_B_OFF = CONV_W
CONV_C_OFF = 2 * CONV_W
SSM_OFF = 3 * CONV_W
Q_OFF = SSM_OFF + SSM_W
K_OFF = Q_OFF + ATTN_W
V_OFF = K_OFF + KV_W
IN_PROJ_W = V_OFF + KV_W

kernel_name = 'hybrid_conv_s5_gqa_diffusion_trunk'


def rms_norm(x, gain, eps=RMS_EPS):
    xf = x.astype(jnp.float32)
    y = xf * lax.rsqrt(jnp.mean(jnp.square(xf), axis=-1, keepdims=True) + eps)
    return (y * gain.astype(jnp.float32)).astype(x.dtype)


def adaln(h, shift, scale):
    return h * (1 + scale) + shift


def axial_rope_angles(rows):
    row = jnp.broadcast_to(jnp.arange(rows)[:, None], (rows, GRID_W)).reshape(-1)
    col = jnp.broadcast_to(jnp.arange(GRID_W)[None, :], (rows, GRID_W)).reshape(-1)
    half = HEAD_DIM // 2
    inv_freq = ROPE_THETA ** (-jnp.arange(0, half, 2, dtype=jnp.float32) / half)
    ang_row = row.astype(jnp.float32)[:, None] * inv_freq
    ang_col = col.astype(jnp.float32)[:, None] * inv_freq
    return ang_row, ang_col


def rotate_half(x, ang):
    m = ang.shape[-1]
    cos = jnp.cos(ang)[:, None, :].astype(x.dtype)
    sin = jnp.sin(ang)[:, None, :].astype(x.dtype)
    x1, x2 = x[..., :m], x[..., m:]
    return jnp.concatenate([x1 * cos - x2 * sin, x1 * sin + x2 * cos], axis=-1)


def apply_axial_rope(x, ang_row, ang_col):
    half = HEAD_DIM // 2
    return jnp.concatenate([rotate_half(x[..., :half], ang_row),
                            rotate_half(x[..., half:], ang_col)], axis=-1)


def short_conv_mixer(z, w):
    v = z[..., CONV_V_OFF:CONV_V_OFF + CONV_W]
    gate_b = z[..., CONV_B_OFF:CONV_B_OFF + CONV_W]
    gate_c = z[..., CONV_C_OFF:CONV_C_OFF + CONV_W]
    pad = CONV_K // 2
    u = jnp.pad(gate_c * v, ((0, 0), (pad, pad), (0, 0)))
    t = z.shape[1]
    y = sum(w[j] * u[:, j:j + t] for j in range(CONV_K))
    return gate_b * y


def s5_discretize(lam_re, lam_im, log_dt, b_re, b_im):
    f32 = jnp.float32
    lam = lax.complex(lam_re.astype(f32), lam_im.astype(f32))
    dt = jnp.exp(log_dt.astype(f32))[:, None]
    lam_bar = jnp.exp(lam * dt)
    b = lax.complex(b_re.astype(f32), b_im.astype(f32))
    b_bar = ((lam_bar - 1.0) / lam)[..., None] * b
    return lam_bar, b_bar


def s5_drive(u, b_bar):
    bsz, t, _ = u.shape
    ug = u.reshape(bsz, t, SSM_GROUPS, SSM_GROUP).astype(jnp.float32).astype(jnp.complex64)
    return jnp.einsum('btgn,gpn->btgp', ug, b_bar)


def s5_scan(lam_bar, bu, h0, reverse):
    if h0 is not None:
        idx = -1 if reverse else 0
        bu = bu.at[:, idx].add(lam_bar * h0)
    a = jnp.broadcast_to(lam_bar, bu.shape)

    def combine(left, right):
        a_l, b_l = left
        a_r, b_r = right
        return a_r * a_l, a_r * b_l + b_r

    _, h = lax.associative_scan(combine, (a, bu), reverse=reverse, axis=1)
    return h


def s5_readout(h, c_re, c_im):
    bsz, t = h.shape[:2]
    f32 = jnp.float32
    y = (jnp.einsum('btgp,gnp->btgn', jnp.real(h), c_re.astype(f32))
         - jnp.einsum('btgp,gnp->btgn', jnp.imag(h), c_im.astype(f32)))
    return y.reshape(bsz, t, SSM_W)


def s5_mixer(u_x, u_c, lam_re, lam_im, log_dt, b_re, b_im, c_re, c_im, d, w_glu, b_glu, want_ctx):
    f32 = jnp.float32
    lam_f, bbar_f = s5_discretize(lam_re[0], lam_im[0], log_dt[0], b_re[0], b_im[0])
    lam_b, bbar_b = s5_discretize(lam_re[1], lam_im[1], log_dt[1], b_re[1], b_im[1])
    hc_f = s5_scan(lam_f, s5_drive(u_c, bbar_f), None, False)
    hc_b = s5_scan(lam_b, s5_drive(u_c, bbar_b), None, True)
    hx_f = s5_scan(lam_f, s5_drive(u_x, bbar_f), hc_f[:, -1], False)
    hx_b = s5_scan(lam_b, s5_drive(u_x, bbar_b), hc_b[:, 0], True)

    def output(h_f, h_b, u):
        y = (s5_readout(h_f, c_re[0], c_im[0]) + s5_readout(h_b, c_re[1], c_im[1])
             + d.astype(f32) * u.astype(f32))
        g = jax.nn.gelu(y)
        gate = jax.nn.sigmoid(g @ w_glu.astype(f32) + b_glu.astype(f32))
        return (g * gate).astype(u.dtype)

    out_x = output(hx_f, hx_b, u_x)
    out_c = output(hc_f, hc_b, u_c) if want_ctx else None
    return out_x, out_c


def heads_q(z, q_gain):
    bsz, t, _ = z.shape
    q = z[..., Q_OFF:Q_OFF + ATTN_W].reshape(bsz, t, N_HEADS, HEAD_DIM)
    return rms_norm(q, q_gain)


def heads_kv(z, k_gain):
    bsz, t, _ = z.shape
    k = rms_norm(z[..., K_OFF:K_OFF + KV_W].reshape(bsz, t, N_KV_HEADS, HEAD_DIM), k_gain)
    v = z[..., V_OFF:V_OFF + KV_W].reshape(bsz, t, N_KV_HEADS, HEAD_DIM)
    return k, v


def gqa_attend(q, k, v):
    s = jnp.einsum('bqkgd,bskd->bkgqs', q, k, preferred_element_type=jnp.float32) * (HEAD_DIM ** -0.5)
    p = jax.nn.softmax(s, axis=-1).astype(v.dtype)
    return jnp.einsum('bkgqs,bskd->bqkgd', p, v)


def attention_mixer(z_x, z_c, q_gain, k_gain, ang_row, ang_col, want_ctx):
    bsz, t, _ = z_x.shape
    n_ctx = z_c.shape[1]
    q_x = apply_axial_rope(heads_q(z_x, q_gain), ang_row, ang_col)
    k_x, v_x = heads_kv(z_x, k_gain)
    k_x = apply_axial_rope(k_x, ang_row, ang_col)
    k_c, v_c = heads_kv(z_c, k_gain)
    k_all = jnp.concatenate([k_x, k_c], axis=1)
    v_all = jnp.concatenate([v_x, v_c], axis=1)
    n_blk = t // Q_BLOCK
    q_blocks = jnp.moveaxis(q_x.reshape(bsz, n_blk, Q_BLOCK, N_KV_HEADS, GQA_GROUP, HEAD_DIM), 1, 0)
    o = lax.map(lambda qb: gqa_attend(qb, k_all, v_all), q_blocks)
    out_x = jnp.moveaxis(o, 0, 1).reshape(bsz, t, ATTN_W)
    out_c = None
    if want_ctx:
        q_c = heads_q(z_c, q_gain).reshape(bsz, n_ctx, N_KV_HEADS, GQA_GROUP, HEAD_DIM)
        out_c = gqa_attend(q_c, k_c, v_c).reshape(bsz, n_ctx, ATTN_W)
    return out_x, out_c


def swiglu(h, w_gate, w_up, w_down):
    return (jax.nn.silu(h @ w_gate) * (h @ w_up)) @ w_down


def setup_inputs(seed: int = 0) -> dict:
    key = jax.random.key(seed)
    ks = jax.random.split(key, 32)
    f32 = jnp.float32

    def nrm(k, shape, scale):
        return jax.random.normal(k, shape, f32) * scale

    def gain(k, shape):
        return 1.0 + 0.02 * jax.random.normal(k, shape, f32)

    n_idx = jnp.arange(SSM_STATE, dtype=f32)
    ssm_shape = (DEPTH, 2, SSM_GROUPS, SSM_STATE)
    return {
        'x': nrm(ks[0], (BATCH, SEQ, D_MODEL), 1.0),
        'c': nrm(ks[1], (BATCH, D_MODEL), 1.0),
        'ctx': nrm(ks[2], (BATCH, CTX_LEN, D_MODEL), 1.0),
        'c_ctx': nrm(ks[3], (D_MODEL,), 1.0),
        'w_mod': nrm(ks[4], (DEPTH, D_MODEL, N_MOD * D_MODEL), D_MODEL ** -0.5),
        'b_mod': nrm(ks[5], (DEPTH, N_MOD * D_MODEL), 0.02),
        'g_pre_mix': gain(ks[6], (DEPTH, D_MODEL)),
        'g_post_mix': gain(ks[7], (DEPTH, D_MODEL)),
        'g_pre_ffn': gain(ks[8], (DEPTH, D_MODEL)),
        'g_post_ffn': gain(ks[9], (DEPTH, D_MODEL)),
        'w_in': nrm(ks[10], (DEPTH, D_MODEL, IN_PROJ_W), D_MODEL ** -0.5),
        'conv_w': nrm(ks[11], (DEPTH, CONV_K, CONV_W), CONV_K ** -0.5),
        'ssm_lam_re': -0.5 + 0.01 * jax.random.normal(ks[12], ssm_shape, f32),
        'ssm_lam_im': jnp.pi * n_idx + 0.01 * jax.random.normal(ks[13], ssm_shape, f32),
        'ssm_log_dt': jax.random.uniform(ks[14], (DEPTH, 2, SSM_GROUPS), f32,
                                         minval=math.log(0.01), maxval=math.log(0.1)),
        'ssm_b_re': nrm(ks[15], ssm_shape + (SSM_GROUP,), (2 * SSM_GROUP) ** -0.5),
        'ssm_b_im': nrm(ks[16], ssm_shape + (SSM_GROUP,), (2 * SSM_GROUP) ** -0.5),
        'ssm_c_re': nrm(ks[17], (DEPTH, 2, SSM_GROUPS, SSM_GROUP, SSM_STATE), 0.5),
        'ssm_c_im': nrm(ks[18], (DEPTH, 2, SSM_GROUPS, SSM_GROUP, SSM_STATE), 0.5),
        'ssm_d': nrm(ks[19], (DEPTH, SSM_W), 1.0),
        'w_glu': nrm(ks[20], (DEPTH, SSM_W, SSM_W), SSM_W ** -0.5),
        'b_glu': nrm(ks[21], (DEPTH, SSM_W), 0.02),
        'q_norm': gain(ks[22], (DEPTH, HEAD_DIM)),
        'k_norm': gain(ks[23], (DEPTH, HEAD_DIM)),
        'w_out': nrm(ks[24], (DEPTH, MIX_W, D_MODEL), MIX_W ** -0.5),
        'w_gate': nrm(ks[25], (DEPTH, D_MODEL, FFN_HIDDEN), D_MODEL ** -0.5),
        'w_up': nrm(ks[26], (DEPTH, D_MODEL, FFN_HIDDEN), D_MODEL ** -0.5),
        'w_down': nrm(ks[27], (DEPTH, FFN_HIDDEN, D_MODEL), FFN_HIDDEN ** -0.5),
    }


def reference(x, c, ctx, c_ctx, w_mod, b_mod, g_pre_mix, g_post_mix, g_pre_ffn, g_post_ffn,
              w_in, conv_w, ssm_lam_re, ssm_lam_im, ssm_log_dt, ssm_b_re, ssm_b_im,
              ssm_c_re, ssm_c_im, ssm_d, w_glu, b_glu, q_norm, k_norm, w_out,
              w_gate, w_up, w_down):
    bsz, n_tok, _ = x.shape
    rows = n_tok // GRID_W
    ang_row, ang_col = axial_rope_angles(rows)
    silu_c = jax.nn.silu(c)
    silu_cc = jax.nn.silu(c_ctx)
    xc = ctx
    for l in range(DEPTH):
        want_ctx = l < DEPTH - 1
        mod_x = (silu_c @ w_mod[l] + b_mod[l]).reshape(bsz, N_MOD, 1, D_MODEL)
        mod_c = (silu_cc @ w_mod[l] + b_mod[l]).reshape(N_MOD, D_MODEL)

        hx = adaln(rms_norm(x, g_pre_mix[l]), mod_x[:, 0], mod_x[:, 1])
        hc = adaln(rms_norm(xc, g_pre_mix[l]), mod_c[0], mod_c[1])
        zx = hx @ w_in[l]
        zc = hc @ w_in[l]
        conv_x = short_conv_mixer(zx, conv_w[l])
        ssm_x, ssm_c = s5_mixer(zx[..., SSM_OFF:SSM_OFF + SSM_W], zc[..., SSM_OFF:SSM_OFF + SSM_W],
                                ssm_lam_re[l], ssm_lam_im[l], ssm_log_dt[l], ssm_b_re[l], ssm_b_im[l],
                                ssm_c_re[l], ssm_c_im[l], ssm_d[l], w_glu[l], b_glu[l], want_ctx)
        attn_x, attn_c = attention_mixer(zx, zc, q_norm[l], k_norm[l], ang_row, ang_col, want_ctx)
        mix_x = jnp.concatenate([conv_x, ssm_x, attn_x], axis=-1) @ w_out[l]
        x = x + mod_x[:, 2] * rms_norm(mix_x, g_post_mix[l])
        if want_ctx:
            conv_c = short_conv_mixer(zc, conv_w[l])
            mix_c = jnp.concatenate([conv_c, ssm_c, attn_c], axis=-1) @ w_out[l]
            xc = xc + mod_c[2] * rms_norm(mix_c, g_post_mix[l])

        hx = adaln(rms_norm(x, g_pre_ffn[l]), mod_x[:, 3], mod_x[:, 4])
        x = x + mod_x[:, 5] * rms_norm(swiglu(hx, w_gate[l], w_up[l], w_down[l]), g_post_ffn[l])
        if want_ctx:
            hc = adaln(rms_norm(xc, g_pre_ffn[l]), mod_c[3], mod_c[4])
            xc = xc + mod_c[5] * rms_norm(swiglu(hc, w_gate[l], w_up[l], w_down[l]), g_post_ffn[l])
    return x
```

```python
import functools
import math

import jax
import jax.numpy as jnp
from jax import lax
from jax.experimental import pallas as pl
from jax.experimental.pallas import tpu as pltpu

F32 = jnp.float32
BF16 = jnp.bfloat16

GRID_W = 64
N_KV_HEADS = 2
N_MOD = 6
ROPE_THETA = 10000.0
RMS_EPS = 1e-6

LANES = 128
SUBLANES = 8
BF16_ROWS = 16
V7X_VMEM_LIMIT_BYTES = 56 * 1024 * 1024

S5_CHUNK = 8
OCTET = LANES // 16

MOD_ROWS = 8


def _dot(a, b):
    return jnp.dot(a, b, preferred_element_type=F32)


def _rms(x, gain):
    ms = jnp.mean(x * x, axis=-1, keepdims=True)
    return x * lax.rsqrt(ms + RMS_EPS) * gain


def _const_spec(shape):
    nd = len(shape)
    return pl.BlockSpec(shape, lambda *_: (0,) * nd, pipeline_mode=pl.Buffered(1))


def _layer_spec(shape, layer):
    nd = len(shape)
    return pl.BlockSpec((1,) + shape, lambda *_: (layer,) + (0,) * nd, pipeline_mode=pl.Buffered(1))


def _params(semantics):
    return pltpu.CompilerParams(dimension_semantics=semantics, vmem_limit_bytes=V7X_VMEM_LIMIT_BYTES)


def _mod_kernel(c_ref, w_ref, b_ref, o_ref):
    s = jax.nn.silu(c_ref[...]).astype(BF16)
    o_ref[0] = _dot(s, w_ref[0].astype(BF16)) + b_ref[0]


def _modulation(cvec, w_mod, b_mod):
    depth, d, n = w_mod.shape
    r = cvec.shape[0]
    tn = min(n, 1024)
    return pl.pallas_call(
        _mod_kernel,
        grid=(depth, n // tn),
        in_specs=[
            pl.BlockSpec((r, d), lambda l, j: (0, 0)),
            pl.BlockSpec((1, d, tn), lambda l, j: (l, 0, j)),
            pl.BlockSpec((1, 1, tn), lambda l, j: (l, 0, j)),
        ],
        out_specs=pl.BlockSpec((1, r, tn), lambda l, j: (l, 0, j)),
        out_shape=jax.ShapeDtypeStruct((depth, r, n), F32),
        compiler_params=_params(("arbitrary", "arbitrary")),
        name="modulation",
    )(cvec, w_mod, b_mod.reshape(depth, 1, n))


def _swap32(y, lane_lo):
    return jnp.where(lane_lo, pltpu.roll(y, LANES - 32, 1), pltpu.roll(y, 32, 1))


def _inproj_kernel(x_ref, mod_ref, g_ref, w_ref, qg_ref, kg_ref, cos_ref, sin_ref,
                   zc_ref, zs_ref, q_ref, k_ref, v_ref, *, cw, sw, aw, hd, rope):
    x = x_ref[0]
    h = _rms(x, g_ref[0]) * (1.0 + mod_ref[0, 1:2, :]) + mod_ref[0, 0:1, :]
    h = h.astype(BF16)
    kvw = N_KV_HEADS * hd
    q_off = 3 * cw + sw
    k_off = q_off + aw
    v_off = k_off + kvw
    zc_ref[0] = _dot(h, w_ref[0, :, 0:3 * cw]).astype(BF16)
    zs_ref[0] = _dot(h, w_ref[0, :, 3 * cw:q_off]).astype(BF16)
    v_ref[0] = _dot(h, w_ref[0, :, v_off:v_off + kvw]).astype(BF16)

    tm = x.shape[0]
    lane_lo = (lax.broadcasted_iota(jnp.int32, (tm, hd), 1) & 32) == 0
    cos = cos_ref[...]
    sin = sin_ref[...]

    def head(zh, gain, scale):
        y = _rms(zh, gain)
        if rope:
            y = y * cos + _swap32(y, lane_lo) * sin
        return (y * scale).astype(BF16) if scale != 1.0 else y.astype(BF16)

    q = _dot(h, w_ref[0, :, q_off:k_off])
    for i in range(aw // hd):
        q_ref[0, :, i * hd:(i + 1) * hd] = head(q[:, i * hd:(i + 1) * hd], qg_ref[0], hd ** -0.5)
    k = _dot(h, w_ref[0, :, k_off:v_off])
    for i in range(N_KV_HEADS):
        k_ref[0, :, i * hd:(i + 1) * hd] = head(k[:, i * hd:(i + 1) * hd], kg_ref[0], 1.0)


def _inproj(x, mod, gain, w_in, q_gain, k_gain, cos, sin, layer, dims, rope, tm):
    b, t, d = x.shape
    cw, sw, aw, hd = dims
    kvw = N_KV_HEADS * hd
    n = w_in.shape[-1]
    tok = lambda w: pl.BlockSpec((1, tm, w), lambda bi, i: (bi, i, 0))
    tab = pl.BlockSpec((tm, hd), lambda bi, i: (i, 0))
    outs = [(3 * cw), sw, aw, kvw, kvw]
    return pl.pallas_call(
        functools.partial(_inproj_kernel, cw=cw, sw=sw, aw=aw, hd=hd, rope=rope),
        grid=(b, t // tm),
        in_specs=[
            tok(d),
            pl.BlockSpec((1, MOD_ROWS, d), lambda bi, i: (bi, 0, 0)),
            _layer_spec((1, d), layer),
            _layer_spec((d, n), layer),
            _layer_spec((1, hd), layer),
            _layer_spec((1, hd), layer),
            tab, tab,
        ],
        out_specs=[tok(w) for w in outs],
        out_shape=[jax.ShapeDtypeStruct((b, t, w), BF16) for w in outs],
        compiler_params=_params(("arbitrary", "arbitrary")),
        name="inproj",
    )(x, mod, gain, w_in, q_gain, k_gain, cos, sin)


def _s5_kernel(xf_ref, xb_ref, at_ref, bf_ref, bb_ref, cf_ref, cb_ref, lam_ref, h0_ref,
               yf_ref, yb_ref, hfin_ref, sf_ref, sb_ref, hf_ref, hb_ref, *, cb, nb):
    s = pl.program_id(1)
    half = hf_ref.shape[-1] // 2

    @pl.when(s == 0)
    def _():
        hf_ref[...] = h0_ref[0, 0]
        hb_ref[...] = h0_ref[1, 0]

    xf = xf_ref[0]
    xb = xb_ref[0]
    sf_ref[...] = _dot(xf, bf_ref[0, 0])
    sb_ref[...] = _dot(xb, bb_ref[0, 0])

    def scan(s_ref, h_ref, lam, reverse):
        lr = jnp.broadcast_to(lam[:, :half], (nb, half))
        li = jnp.broadcast_to(lam[:, half:], (nb, half))

        def body(k, carry):
            hr, hi = carry
            c = (cb - 1 - k) if reverse else k
            r0 = pl.multiple_of(c * nb, nb)
            srow = s_ref[pl.ds(r0, nb), :]
            s_ref[pl.ds(r0, nb), :] = jnp.concatenate([hr, hi], axis=1)
            nhr = lr * hr - li * hi + srow[:, :half]
            nhi = lr * hi + li * hr + srow[:, half:]
            return nhr, nhi

        hr, hi = lax.fori_loop(0, cb, body, (h_ref[:, :half], h_ref[:, half:]))
        h_ref[...] = jnp.concatenate([hr, hi], axis=1)

    scan(sf_ref, hf_ref, lam_ref[0, 0, 0], False)
    scan(sb_ref, hb_ref, lam_ref[1, 0, 0], True)

    yf_ref[0] = _dot(xf, at_ref[0, 0]) + _dot(sf_ref[...].astype(BF16), cf_ref[0, 0])
    yb_ref[0] = _dot(sb_ref[...].astype(BF16), cb_ref[0, 0])

    @pl.when(s == pl.num_programs(1) - 1)
    def _():
        hfin_ref[0, 0] = hf_ref[...]
        hfin_ref[1, 0] = hb_ref[...]


def _s5(xq, ops, h0, layer, nb, cb):
    a_tot, b_f, b_b, c_f, c_b, lam8 = ops
    n_oct, rows, kx = xq.shape
    ns = a_tot.shape[-1]
    sw = lam8.shape[-1]
    nblk = rows // (cb * nb)
    blk = cb * nb
    fwd = lambda o, s: (o, s, 0)
    bwd = lambda o, s: (o, nblk - 1 - s, 0)
    op_spec = lambda r, w: pl.BlockSpec((1, 1, r, w), lambda o, s: (layer, o, 0, 0),
                                        pipeline_mode=pl.Buffered(1))
    return pl.pallas_call(
        functools.partial(_s5_kernel, cb=cb, nb=nb),
        grid=(n_oct, nblk),
        in_specs=[
            pl.BlockSpec((1, blk, kx), fwd),
            pl.BlockSpec((1, blk, kx), bwd),
            op_spec(kx, ns), op_spec(kx, sw), op_spec(kx, sw), op_spec(sw, ns), op_spec(sw, ns),
            pl.BlockSpec((2, 1, 1, 1, sw), lambda o, s: (0, layer, o, 0, 0)),
            pl.BlockSpec((2, 1, nb, sw), lambda o, s: (0, o, 0, 0)),
        ],
        out_specs=[
            pl.BlockSpec((1, blk, ns), fwd),
            pl.BlockSpec((1, blk, ns), bwd),
            pl.BlockSpec((2, 1, nb, sw), lambda o, s: (0, o, 0, 0)),
        ],
        out_shape=[
            jax.ShapeDtypeStruct((n_oct, rows, ns), F32),
            jax.ShapeDtypeStruct((n_oct, rows, ns), F32),
            jax.ShapeDtypeStruct((2, n_oct, nb, sw), F32),
        ],
        scratch_shapes=[
            pltpu.VMEM((blk, sw), F32), pltpu.VMEM((blk, sw), F32),
            pltpu.VMEM((nb, sw), F32), pltpu.VMEM((nb, sw), F32),
        ],
        compiler_params=_params(("arbitrary", "arbitrary")),
        name="s5_mixer",
    )(xq, xq, a_tot, b_f, b_b, c_f, c_b, lam8, h0)


def _cmul(a, b):
    return a[0] * b[0] - a[1] * b[1], a[0] * b[1] + a[1] * b[0]


def _s5_operators(lam_re, lam_im, log_dt, b_re, b_im, c_re, c_im):
    hp = lax.Precision.HIGHEST
    depth, _, g, p = lam_re.shape
    n = b_re.shape[-1]
    n_oct = g // OCTET
    m = S5_CHUNK
    dt = jnp.exp(log_dt)[..., None]
    mag = jnp.exp(lam_re * dt)
    lam_bar = (mag * jnp.cos(lam_im * dt), mag * jnp.sin(lam_im * dt))
    den = lam_re * lam_re + lam_im * lam_im
    num = (lam_bar[0] - 1.0, lam_bar[1])
    ratio = ((num[0] * lam_re + num[1] * lam_im) / den, (num[1] * lam_re - num[0] * lam_im) / den)
    bbar = _cmul((ratio[0][..., None], ratio[1][..., None]), (b_re, b_im))
    pows = [(jnp.ones_like(lam_re), jnp.zeros_like(lam_re))]
    for _ in range(m):
        pows.append(_cmul(pows[-1], lam_bar))
    pw = (jnp.stack([q[0] for q in pows]), jnp.stack([q[1] for q in pows]))
    eye = jnp.eye(OCTET, dtype=F32)

    def expand_in(e, d, exps):
        sel = jnp.stack([e[k][:, d] for k in exps])
        sel = sel.reshape(m, depth, n_oct, OCTET, p, n)
        out = jnp.einsum('ilogpn,gh->loignhp', sel, eye, precision=hp)
        return out.reshape(depth, n_oct, m * OCTET * n, OCTET * p)

    def expand_out(f, d, exps):
        sel = jnp.stack([f[k][:, d] for k in exps]).reshape(m, depth, n_oct, OCTET, n, p)
        out = jnp.einsum('jlogqp,gh->logpjhq', sel, eye, precision=hp)
        return out.reshape(depth, n_oct, OCTET * p, m * OCTET * n)

    e = [_cmul((pw[0][k][..., None], pw[1][k][..., None]), bbar) for k in range(m + 1)]
    f = [_cmul((c_re, c_im), (pw[0][k][:, :, :, None, :], pw[1][k][:, :, :, None, :])) for k in range(m + 1)]
    er, ei = [q[0] for q in e], [q[1] for q in e]
    fr, fi = [q[0] for q in f], [-q[1] for q in f]
    fexp = [m - 1 - i for i in range(m)]
    bexp = list(range(m))
    b_f = jnp.concatenate([expand_in(er, 0, fexp), expand_in(ei, 0, fexp)], axis=-1)
    b_b = jnp.concatenate([expand_in(er, 1, bexp), expand_in(ei, 1, bexp)], axis=-1)
    c_f = jnp.concatenate([expand_out(fr, 0, [j + 1 for j in range(m)]),
                           expand_out(fi, 0, [j + 1 for j in range(m)])], axis=-2)
    c_b = jnp.concatenate([expand_out(fr, 1, [m - j for j in range(m)]),
                           expand_out(fi, 1, [m - j for j in range(m)])], axis=-2)

    kk = jnp.stack([
        jnp.einsum('ldgop,ldgpn->ldgon', f[k][0], bbar[0], precision=hp)
        - jnp.einsum('ldgop,ldgpn->ldgon', f[k][1], bbar[1], precision=hp)
        for k in range(m)])
    tau = jnp.arange(m)[None, :] - jnp.arange(m)[:, None]
    kf = kk[:, :, 0][jnp.clip(tau, 0, m - 1)]
    kb = kk[:, :, 1][jnp.clip(-tau, 0, m - 1)]
    tmask = lambda c: c[:, :, None, None, None, None]
    toe = jnp.where(tmask(tau >= 0), kf, 0.0) + jnp.where(tmask(tau <= 0), kb, 0.0)
    toe = toe.reshape(m, m, depth, n_oct, OCTET, n, n)
    a_tot = jnp.einsum('ijlogqn,gh->loignjhq', toe, eye, precision=hp)
    a_tot = a_tot.reshape(depth, n_oct, m * OCTET * n, m * OCTET * n)

    lam8 = jnp.concatenate([pw[0][m].reshape(depth, 2, n_oct, 1, OCTET * p),
                            pw[1][m].reshape(depth, 2, n_oct, 1, OCTET * p)], axis=-1)
    lam8 = jnp.moveaxis(lam8, 1, 0)
    return (a_tot.astype(BF16), b_f.astype(BF16), b_b.astype(BF16),
            c_f.astype(BF16), c_b.astype(BF16), lam8)


def _to_chunk_layout(u):
    b, t, w = u.shape
    n_oct = w // LANES
    u = u.reshape(b, t // S5_CHUNK, S5_CHUNK, n_oct, LANES)
    u = jnp.transpose(u, (3, 1, 0, 2, 4))
    return u.reshape(n_oct, (t // S5_CHUNK) * b, S5_CHUNK * LANES)


def _from_chunk_layout(y, b):
    n_oct, rows, _ = y.shape
    mc = rows // b
    y = y.reshape(n_oct, mc, b, S5_CHUNK, LANES)
    y = jnp.transpose(y, (2, 1, 3, 0, 4))
    return y.reshape(b, mc * S5_CHUNK, n_oct * LANES)


def _attn_kernel(*refs, group, hd, chunks):
    q_ref = refs[0]
    o_ref = refs[-1]
    kv = refs[1:-1]
    tq = q_ref.shape[1]
    q = q_ref[0]
    qs = jnp.concatenate([q[:, h * hd:(h + 1) * hd] for h in range(group)], axis=0)
    rows = group * tq
    carry = (jnp.full((rows, 1), -jnp.inf, F32), jnp.zeros((rows, 1), F32), jnp.zeros((rows, hd), F32))
    for i, tk in enumerate(chunks):
        k_ref, v_ref = kv[2 * i], kv[2 * i + 1]

        def body(c, carry, k_ref=k_ref, v_ref=v_ref, tk=tk):
            m, l, acc = carry
            r0 = pl.multiple_of(c * tk, tk)
            k = k_ref[0, pl.ds(r0, tk), :]
            v = v_ref[0, pl.ds(r0, tk), :]
            s = lax.dot_general(qs, k, (((1,), (1,)), ((), ())), preferred_element_type=F32)
            m_new = jnp.maximum(m, jnp.max(s, axis=-1, keepdims=True))
            alpha = jnp.exp(m - m_new)
            p = jnp.exp(s - m_new)
            l = alpha * l + jnp.sum(p, axis=-1, keepdims=True)
            acc = alpha * acc + _dot(p.astype(BF16), v)
            return m_new, l, acc

        carry = lax.fori_loop(0, k_ref.shape[1] // tk, body, carry)
    _, l, acc = carry
    o = acc / l
    for h in range(group):
        o_ref[0, :, h * hd:(h + 1) * hd] = o[h * tq:(h + 1) * tq].astype(BF16)


def _attention(q, kvs, hd, tq):
    b, t, aw = q.shape
    group = aw // hd // N_KV_HEADS
    gw = group * hd
    chunks = tuple(min(k.shape[1], 512) for k, _ in kvs)
    in_specs = [pl.BlockSpec((1, tq, gw), lambda bi, j, i: (bi, i, j))]
    args = [q]
    for k, v in kvs:
        s = k.shape[1]
        in_specs += [pl.BlockSpec((1, s, hd), lambda bi, j, i: (bi, 0, j))] * 2
        args += [k, v]
    return pl.pallas_call(
        functools.partial(_attn_kernel, group=group, hd=hd, chunks=chunks),
        grid=(b, N_KV_HEADS, t // tq),
        in_specs=in_specs,
        out_specs=pl.BlockSpec((1, tq, gw), lambda bi, j, i: (bi, i, j)),
        out_shape=jax.ShapeDtypeStruct((b, t, aw), BF16),
        compiler_params=_params(("arbitrary", "arbitrary", "arbitrary")),
        name="attention",
    )(*args)


def _outproj_kernel(zc_ref, zp_ref, zn_ref, yf_ref, yb_ref, u_ref, at_ref, x_ref, mod_ref,
                    cwt_ref, d_ref, wglu_ref, bglu_ref, wo_ref, g_ref, o_ref, ubuf_ref, *, cw):
    i = pl.program_id(1)
    tm = x_ref.shape[1]

    def gated(z):
        z = z.astype(F32)
        return z[:, 2 * cw:] * z[:, :cw]

    zc = zc_ref[0].astype(F32)
    u = zc[:, 2 * cw:] * zc[:, :cw]
    prev = gated(zp_ref[0])[BF16_ROWS - 1:BF16_ROWS]
    nxt = gated(zn_ref[0])[0:1]
    ubuf_ref[SUBLANES - 1:SUBLANES, :] = jnp.where(i > 0, prev, 0.0)
    ubuf_ref[SUBLANES:SUBLANES + tm, :] = u
    ubuf_ref[SUBLANES + tm:SUBLANES + tm + 1, :] = jnp.where(i < pl.num_programs(1) - 1, nxt, 0.0)
    y = (cwt_ref[0, 0:1] * ubuf_ref[SUBLANES - 1:SUBLANES - 1 + tm, :] + cwt_ref[0, 1:2] * u
         + cwt_ref[0, 2:3] * ubuf_ref[SUBLANES + 1:SUBLANES + 1 + tm, :])
    conv = (zc[:, cw:2 * cw] * y).astype(BF16)

    ys = yf_ref[0] + yb_ref[0] + d_ref[0] * u_ref[0].astype(F32)
    gl = jax.nn.gelu(ys)
    gate = jax.nn.sigmoid(_dot(gl.astype(BF16), wglu_ref[0]) + bglu_ref[0])
    ssm = (gl * gate).astype(BF16)

    mix = _dot(jnp.concatenate([conv, ssm, at_ref[0]], axis=1), wo_ref[0])
    o_ref[0] = x_ref[0] + mod_ref[0, 2:3, :] * _rms(mix, g_ref[0])


def _outproj(zc, yf, yb, u, at, x, mod, conv_w, ssm_d, w_glu, b_glu, w_out, gain, layer, dims, tm):
    b, t, d = x.shape
    cw, sw, aw, hd = dims
    nhalo = tm // BF16_ROWS
    last = t // BF16_ROWS - 1
    tok = lambda w: pl.BlockSpec((1, tm, w), lambda bi, i: (bi, i, 0))
    return pl.pallas_call(
        functools.partial(_outproj_kernel, cw=cw),
        grid=(b, t // tm),
        in_specs=[
            tok(3 * cw),
            pl.BlockSpec((1, BF16_ROWS, 3 * cw), lambda bi, i: (bi, jnp.maximum(i * nhalo - 1, 0), 0)),
            pl.BlockSpec((1, BF16_ROWS, 3 * cw), lambda bi, i: (bi, jnp.minimum((i + 1) * nhalo, last), 0)),
            tok(sw), tok(sw), tok(sw), tok(aw), tok(d),
            pl.BlockSpec((1, MOD_ROWS, d), lambda bi, i: (bi, 0, 0)),
            _layer_spec((conv_w.shape[1], cw), layer),
            _layer_spec((1, sw), layer),
            _layer_spec((sw, sw), layer),
            _layer_spec((1, sw), layer),
            _layer_spec((d, d), layer),
            _layer_spec((1, d), layer),
        ],
        out_specs=tok(d),
        out_shape=jax.ShapeDtypeStruct((b, t, d), F32),
        scratch_shapes=[pltpu.VMEM((tm + 2 * SUBLANES, cw), F32)],
        input_output_aliases={7: 0},
        compiler_params=_params(("arbitrary", "arbitrary")),
        name="outproj",
    )(zc, zc, zc, yf, yb, u, at, x, mod, conv_w, ssm_d, w_glu, b_glu, w_out, gain)


def _ffn_kernel(x_ref, mod_ref, gpre_ref, gpost_ref, wg_ref, wu_ref, wd_ref, o_ref, h_ref, acc_ref):
    j = pl.program_id(2)

    @pl.when(j == 0)
    def _():
        h = _rms(x_ref[0], gpre_ref[0]) * (1.0 + mod_ref[0, 4:5, :]) + mod_ref[0, 3:4, :]
        h_ref[...] = h.astype(BF16)
        acc_ref[...] = jnp.zeros_like(acc_ref)

    h = h_ref[...]
    a = jax.nn.silu(_dot(h, wg_ref[0])) * _dot(h, wu_ref[0])
    acc_ref[...] += _dot(a.astype(BF16), wd_ref[0])

    @pl.when(j == pl.num_programs(2) - 1)
    def _():
        o_ref[0] = x_ref[0] + mod_ref[0, 5:6, :] * _rms(acc_ref[...], gpost_ref[0])


def _ffn(x, mod, g_pre, g_post, w_gate, w_up, w_down, layer, tm, th):
    b, t, d = x.shape
    f = w_gate.shape[-1]
    tok = pl.BlockSpec((1, tm, d), lambda bi, i, j: (bi, i, 0))
    return pl.pallas_call(
        _ffn_kernel,
        grid=(b, t // tm, f // th),
        in_specs=[
            tok,
            pl.BlockSpec((1, MOD_ROWS, d), lambda bi, i, j: (bi, 0, 0)),
            _layer_spec((1, d), layer),
            _layer_spec((1, d), layer),
            pl.BlockSpec((1, d, th), lambda bi, i, j: (layer, 0, j)),
            pl.BlockSpec((1, d, th), lambda bi, i, j: (layer, 0, j)),
            pl.BlockSpec((1, th, d), lambda bi, i, j: (layer, j, 0)),
        ],
        out_specs=tok,
        out_shape=jax.ShapeDtypeStruct((b, t, d), F32),
        scratch_shapes=[pltpu.VMEM((tm, d), BF16), pltpu.VMEM((tm, d), F32)],
        input_output_aliases={0: 0},
        compiler_params=_params(("arbitrary", "arbitrary", "arbitrary")),
        name="ffn",
    )(x, mod, g_pre, g_post, w_gate, w_up, w_down)


def _rope_tables(t, hd):
    rows = t // GRID_W
    row = jnp.broadcast_to(jnp.arange(rows)[:, None], (rows, GRID_W)).reshape(-1).astype(F32)
    col = jnp.broadcast_to(jnp.arange(GRID_W)[None, :], (rows, GRID_W)).reshape(-1).astype(F32)
    half = hd // 2
    inv_freq = ROPE_THETA ** (-jnp.arange(0, half, 2, dtype=F32) / half)
    ar = row[:, None] * inv_freq
    ac = col[:, None] * inv_freq
    cos = jnp.concatenate([jnp.cos(ar), jnp.cos(ar), jnp.cos(ac), jnp.cos(ac)], axis=-1)
    sin = jnp.concatenate([-jnp.sin(ar), jnp.sin(ar), -jnp.sin(ac), jnp.sin(ac)], axis=-1)
    return cos, sin


def _tile(n, pref):
    return pref if n % pref == 0 else n


def kernel(x, c, ctx, c_ctx, w_mod, b_mod, g_pre_mix, g_post_mix, g_pre_ffn, g_post_ffn, w_in, conv_w, ssm_lam_re, ssm_lam_im, ssm_log_dt, ssm_b_re, ssm_b_im, ssm_c_re, ssm_c_im, ssm_d, w_glu, b_glu, q_norm, k_norm, w_out, w_gate, w_up, w_down):
    bsz, t, d = x.shape
    n_ctx = ctx.shape[1]
    depth = w_mod.shape[0]
    cw = conv_w.shape[-1]
    sw = w_glu.shape[-1]
    hd = q_norm.shape[-1]
    aw = d - cw - sw
    dims = (cw, sw, aw, hd)
    assert w_in.shape[-1] == 3 * cw + sw + aw + 2 * N_KV_HEADS * hd
    assert sw % LANES == 0 and t % (S5_CHUNK * BF16_ROWS) == 0 and n_ctx % (S5_CHUNK * BF16_ROWS) == 0

    cvec = jnp.concatenate([c, c_ctx[None, :], jnp.zeros((BF16_ROWS - bsz - 1, d), F32)], axis=0)
    mod = _modulation(cvec, w_mod, b_mod).reshape(depth, BF16_ROWS, N_MOD, d)
    mod = jnp.pad(mod, ((0, 0), (0, 0), (0, MOD_ROWS - N_MOD), (0, 0)))
    mod_x = mod[:, :bsz]
    mod_c = jnp.broadcast_to(mod[:, bsz:bsz + 1], (depth, bsz, MOD_ROWS, d))

    cos, sin = _rope_tables(t, hd)
    ops = _s5_operators(ssm_lam_re, ssm_lam_im, ssm_log_dt, ssm_b_re, ssm_b_im, ssm_c_re, ssm_c_im)
    n_oct = sw // LANES

    r3 = lambda a: a.reshape(depth, 1, a.shape[-1])
    g_pre_mix, g_post_mix, g_pre_ffn, g_post_ffn = map(r3, (g_pre_mix, g_post_mix, g_pre_ffn, g_post_ffn))
    q_norm, k_norm, ssm_d, b_glu = map(r3, (q_norm, k_norm, ssm_d, b_glu))
    w_in, w_glu, w_out, w_gate, w_up, w_down = (w.astype(BF16) for w in (w_in, w_glu, w_out, w_gate, w_up, w_down))

    tm_x, tm_c = _tile(t, 512), _tile(n_ctx, 256)
    tq_x, tq_c = _tile(t, 256), _tile(n_ctx, 256)
    th = _tile(w_gate.shape[-1], 512)
    cb_x = _tile(t // S5_CHUNK, 64)
    cb_c = n_ctx // S5_CHUNK
    h_zero = jnp.zeros((2, n_oct, bsz, ops[-1].shape[-1]), F32)

    def s5(zs, h0, layer, cb):
        yf, yb, hfin = _s5(_to_chunk_layout(zs), ops, h0, layer, bsz, cb)
        return _from_chunk_layout(yf, bsz), _from_chunk_layout(yb, bsz), hfin

    xc = ctx
    for l in range(depth):
        want_ctx = l < depth - 1
        zc_x, zs_x, q_x, k_x, v_x = _inproj(x, mod_x[l], g_pre_mix, w_in, q_norm, k_norm, cos, sin, l, dims, True, tm_x)
        zc_c, zs_c, q_c, k_c, v_c = _inproj(xc, mod_c[l], g_pre_mix, w_in, q_norm, k_norm, cos, sin, l, dims, False, tm_c)
        yf_c, yb_c, h_ctx = s5(zs_c, h_zero, l, cb_c)
        yf_x, yb_x, _ = s5(zs_x, h_ctx, l, cb_x)
        at_x = _attention(q_x, [(k_x, v_x), (k_c, v_c)], hd, tq_x)
        x = _outproj(zc_x, yf_x, yb_x, zs_x, at_x, x, mod_x[l], conv_w, ssm_d, w_glu, b_glu, w_out,
                     g_post_mix, l, dims, tm_x)
        x = _ffn(x, mod_x[l], g_pre_ffn, g_post_ffn, w_gate, w_up, w_down, l, tm_x, th)
        if want_ctx:
            at_c = _attention(q_c, [(k_c, v_c)], hd, tq_c)
            xc = _outproj(zc_c, yf_c, yb_c, zs_c, at_c, xc, mod_c[l], conv_w, ssm_d, w_glu, b_glu, w_out,
                          g_post_mix, l, dims, tm_c)
            xc = _ffn(xc, mod_c[l], g_pre_ffn, g_post_ffn, w_gate, w_up, w_down, l, tm_c, th)
    return x
```

```python
import functools
import math

import jax
import jax.numpy as jnp
from jax import lax
from jax.experimental import pallas as pl
from jax.experimental.pallas import tpu as pltpu

F32 = jnp.float32
BF16 = jnp.bfloat16

GRID_W = 64
N_KV_HEADS = 2
N_MOD = 6
ROPE_THETA = 10000.0
RMS_EPS = 1e-6

LANES = 128
SUBLANES = 8
BF16_ROWS = 16
V7X_VMEM_LIMIT_BYTES = 56 * 1024 * 1024

S5_CHUNK = 8
OCTET = LANES // 16

MOD_ROWS = 8


def _dot(a, b):
    return jnp.dot(a, b, preferred_element_type=F32)


def _rms(x, gain):
    ms = jnp.mean(x * x, axis=-1, keepdims=True)
    return x * lax.rsqrt(ms + RMS_EPS) * gain


def _const_spec(shape):
    nd = len(shape)
    return pl.BlockSpec(shape, lambda *_: (0,) * nd, pipeline_mode=pl.Buffered(1))


def _layer_spec(shape, layer):
    nd = len(shape)
    return pl.BlockSpec((1,) + shape, lambda *_: (layer,) + (0,) * nd, pipeline_mode=pl.Buffered(1))


def _params(semantics):
    return pltpu.CompilerParams(dimension_semantics=semantics, vmem_limit_bytes=V7X_VMEM_LIMIT_BYTES)


def _mod_kernel(c_ref, w_ref, b_ref, o_ref):
    s = jax.nn.silu(c_ref[...]).astype(BF16)
    o_ref[0] = _dot(s, w_ref[0].astype(BF16)) + b_ref[0]


def _modulation(cvec, w_mod, b_mod):
    depth, d, n = w_mod.shape
    r = cvec.shape[0]
    tn = min(n, 1024)
    return pl.pallas_call(
        _mod_kernel,
        grid=(depth, n // tn),
        in_specs=[
            pl.BlockSpec((r, d), lambda l, j: (0, 0)),
            pl.BlockSpec((1, d, tn), lambda l, j: (l, 0, j)),
            pl.BlockSpec((1, 1, tn), lambda l, j: (l, 0, j)),
        ],
        out_specs=pl.BlockSpec((1, r, tn), lambda l, j: (l, 0, j)),
        out_shape=jax.ShapeDtypeStruct((depth, r, n), F32),
        compiler_params=_params(("arbitrary", "arbitrary")),
        name="modulation",
    )(cvec, w_mod, b_mod.reshape(depth, 1, n))


def _swap32(y, lane_lo):
    return jnp.where(lane_lo, pltpu.roll(y, LANES - 32, 1), pltpu.roll(y, 32, 1))


def _swap32_rows(y):
    return jnp.concatenate([y[32:64], y[0:32], y[96:128], y[64:96]], axis=0)


def _inproj_kernel(x_ref, mod_ref, g_ref, wa_ref, wbt_ref, qg_ref, kg_ref, cos_ref, sin_ref, cost_ref, sint_ref,
                   zc_ref, zs_ref, qt_ref, k_ref, vt_ref, *, cw, sw, aw, hd, rope):
    x = x_ref[0]
    h = _rms(x, g_ref[0]) * (1.0 + mod_ref[0, 1:2, :]) + mod_ref[0, 0:1, :]
    h = h.astype(BF16)
    kvw = N_KV_HEADS * hd
    k_off = 3 * cw + sw
    zc_ref[0] = _dot(h, wa_ref[0, :, 0:3 * cw]).astype(BF16)
    zs_ref[0] = _dot(h, wa_ref[0, :, 3 * cw:k_off]).astype(BF16)

    tm = x.shape[0]
    nt = (((1,), (1,)), ((), ()))
    vt_ref[0] = lax.dot_general(wbt_ref[0, aw:aw + kvw, :], h, nt, preferred_element_type=F32).astype(BF16)

    lane_lo = (lax.broadcasted_iota(jnp.int32, (tm, hd), 1) & 32) == 0
    k = _dot(h, wa_ref[0, :, k_off:k_off + kvw])
    for i in range(N_KV_HEADS):
        y = _rms(k[:, i * hd:(i + 1) * hd], kg_ref[0])
        if rope:
            y = y * cos_ref[...] + _swap32(y, lane_lo) * sin_ref[...]
        k_ref[0, :, i * hd:(i + 1) * hd] = y.astype(BF16)

    qscale = hd ** -0.5 * math.log2(math.e)
    qt = lax.dot_general(wbt_ref[0, 0:aw, :], h, nt, preferred_element_type=F32)
    for i in range(aw // hd):
        z = qt[i * hd:(i + 1) * hd, :]
        ms = jnp.mean(z * z, axis=0, keepdims=True)
        y = z * lax.rsqrt(ms + RMS_EPS) * qg_ref[0]
        if rope:
            y = y * cost_ref[...] + _swap32_rows(y) * sint_ref[...]
        qt_ref[0, i * hd:(i + 1) * hd, :] = (y * qscale).astype(BF16)


def _inproj(x, mod, gain, w_a, w_bt, q_gain_col, k_gain, tabs, layer, dims, rope, tm):
    b, t, d = x.shape
    cw, sw, aw, hd = dims
    kvw = N_KV_HEADS * hd
    cos, sin, cos_t, sin_t = tabs
    tok = lambda w: pl.BlockSpec((1, tm, w), lambda bi, i: (bi, i, 0))
    tok_t = lambda w: pl.BlockSpec((1, w, tm), lambda bi, i: (bi, 0, i))
    tab = pl.BlockSpec((tm, hd), lambda bi, i: (i, 0))
    tab_t = pl.BlockSpec((hd, tm), lambda bi, i: (0, i))
    sds = jax.ShapeDtypeStruct
    return pl.pallas_call(
        functools.partial(_inproj_kernel, cw=cw, sw=sw, aw=aw, hd=hd, rope=rope),
        grid=(b, t // tm),
        in_specs=[
            tok(d),
            pl.BlockSpec((1, MOD_ROWS, d), lambda bi, i: (bi, 0, 0)),
            _layer_spec((1, d), layer),
            _layer_spec((d, w_a.shape[-1]), layer),
            _layer_spec((aw + kvw, d), layer),
            _layer_spec((hd, 1), layer),
            _layer_spec((1, hd), layer),
            tab, tab, tab_t, tab_t,
        ],
        out_specs=[tok(3 * cw), tok(sw), tok_t(aw), tok(kvw), tok_t(kvw)],
        out_shape=[sds((b, t, 3 * cw), BF16), sds((b, t, sw), BF16), sds((b, aw, t), BF16),
                   sds((b, t, kvw), BF16), sds((b, kvw, t), BF16)],
        compiler_params=_params(("arbitrary", "arbitrary")),
        name="inproj",
    )(x, mod, gain, w_a, w_bt, q_gain_col, k_gain, cos, sin, cos_t, sin_t)


def _s5_kernel(xf_ref, xb_ref, at_ref, bf_ref, bb_ref, cf_ref, cb_ref, lam_ref, h0_ref,
               yf_ref, yb_ref, hfin_ref, sf_ref, sb_ref, hf_ref, hb_ref, *, cb, nb):
    s = pl.program_id(1)
    half = hf_ref.shape[-1] // 2

    @pl.when(s == 0)
    def _():
        hf_ref[...] = h0_ref[0, 0]
        hb_ref[...] = h0_ref[1, 0]

    xf = xf_ref[0]
    xb = xb_ref[0]
    sf_ref[...] = _dot(xf, bf_ref[0, 0])
    sb_ref[...] = _dot(xb, bb_ref[0, 0])

    def scan(s_ref, h_ref, lam, reverse):
        lr = jnp.broadcast_to(lam[:, :half], (nb, half))
        li = jnp.broadcast_to(lam[:, half:], (nb, half))

        def body(k, carry):
            hr, hi = carry
            c = (cb - 1 - k) if reverse else k
            r0 = pl.multiple_of(c * nb, nb)
            srow = s_ref[pl.ds(r0, nb), :]
            s_ref[pl.ds(r0, nb), :] = jnp.concatenate([hr, hi], axis=1)
            nhr = lr * hr - li * hi + srow[:, :half]
            nhi = lr * hi + li * hr + srow[:, half:]
            return nhr, nhi

        hr, hi = lax.fori_loop(0, cb, body, (h_ref[:, :half], h_ref[:, half:]))
        h_ref[...] = jnp.concatenate([hr, hi], axis=1)

    scan(sf_ref, hf_ref, lam_ref[0, 0, 0], False)
    scan(sb_ref, hb_ref, lam_ref[1, 0, 0], True)

    yf_ref[0] = _dot(xf, at_ref[0, 0]) + _dot(sf_ref[...].astype(BF16), cf_ref[0, 0])
    yb_ref[0] = _dot(sb_ref[...].astype(BF16), cb_ref[0, 0])

    @pl.when(s == pl.num_programs(1) - 1)
    def _():
        hfin_ref[0, 0] = hf_ref[...]
        hfin_ref[1, 0] = hb_ref[...]


def _s5(xq, ops, h0, layer, nb, cb):
    a_tot, b_f, b_b, c_f, c_b, lam8 = ops
    n_oct, rows, kx = xq.shape
    ns = a_tot.shape[-1]
    sw = lam8.shape[-1]
    nblk = rows // (cb * nb)
    blk = cb * nb
    fwd = lambda o, s: (o, s, 0)
    bwd = lambda o, s: (o, nblk - 1 - s, 0)
    op_spec = lambda r, w: pl.BlockSpec((1, 1, r, w), lambda o, s: (layer, o, 0, 0),
                                        pipeline_mode=pl.Buffered(1))
    return pl.pallas_call(
        functools.partial(_s5_kernel, cb=cb, nb=nb),
        grid=(n_oct, nblk),
        in_specs=[
            pl.BlockSpec((1, blk, kx), fwd),
            pl.BlockSpec((1, blk, kx), bwd),
            op_spec(kx, ns), op_spec(kx, sw), op_spec(kx, sw), op_spec(sw, ns), op_spec(sw, ns),
            pl.BlockSpec((2, 1, 1, 1, sw), lambda o, s: (0, layer, o, 0, 0)),
            pl.BlockSpec((2, 1, nb, sw), lambda o, s: (0, o, 0, 0)),
        ],
        out_specs=[
            pl.BlockSpec((1, blk, ns), fwd),
            pl.BlockSpec((1, blk, ns), bwd),
            pl.BlockSpec((2, 1, nb, sw), lambda o, s: (0, o, 0, 0)),
        ],
        out_shape=[
            jax.ShapeDtypeStruct((n_oct, rows, ns), F32),
            jax.ShapeDtypeStruct((n_oct, rows, ns), F32),
            jax.ShapeDtypeStruct((2, n_oct, nb, sw), F32),
        ],
        scratch_shapes=[
            pltpu.VMEM((blk, sw), F32), pltpu.VMEM((blk, sw), F32),
            pltpu.VMEM((nb, sw), F32), pltpu.VMEM((nb, sw), F32),
        ],
        compiler_params=_params(("arbitrary", "arbitrary")),
        name="s5_mixer",
    )(xq, xq, a_tot, b_f, b_b, c_f, c_b, lam8, h0)


def _cmul(a, b):
    return a[0] * b[0] - a[1] * b[1], a[0] * b[1] + a[1] * b[0]


def _s5_operators(lam_re, lam_im, log_dt, b_re, b_im, c_re, c_im):
    hp = lax.Precision.HIGHEST
    depth, _, g, p = lam_re.shape
    n = b_re.shape[-1]
    n_oct = g // OCTET
    m = S5_CHUNK
    dt = jnp.exp(log_dt)[..., None]
    mag = jnp.exp(lam_re * dt)
    lam_bar = (mag * jnp.cos(lam_im * dt), mag * jnp.sin(lam_im * dt))
    den = lam_re * lam_re + lam_im * lam_im
    num = (lam_bar[0] - 1.0, lam_bar[1])
    ratio = ((num[0] * lam_re + num[1] * lam_im) / den, (num[1] * lam_re - num[0] * lam_im) / den)
    bbar = _cmul((ratio[0][..., None], ratio[1][..., None]), (b_re, b_im))
    pows = [(jnp.ones_like(lam_re), jnp.zeros_like(lam_re))]
    for _ in range(m):
        pows.append(_cmul(pows[-1], lam_bar))
    pw = (jnp.stack([q[0] for q in pows]), jnp.stack([q[1] for q in pows]))
    eye = jnp.eye(OCTET, dtype=F32)

    def expand_in(e, d, exps):
        sel = jnp.stack([e[k][:, d] for k in exps])
        sel = sel.reshape(m, depth, n_oct, OCTET, p, n)
        out = jnp.einsum('ilogpn,gh->loignhp', sel, eye, precision=hp)
        return out.reshape(depth, n_oct, m * OCTET * n, OCTET * p)

    def expand_out(f, d, exps):
        sel = jnp.stack([f[k][:, d] for k in exps]).reshape(m, depth, n_oct, OCTET, n, p)
        out = jnp.einsum('jlogqp,gh->logpjhq', sel, eye, precision=hp)
        return out.reshape(depth, n_oct, OCTET * p, m * OCTET * n)

    e = [_cmul((pw[0][k][..., None], pw[1][k][..., None]), bbar) for k in range(m + 1)]
    f = [_cmul((c_re, c_im), (pw[0][k][:, :, :, None, :], pw[1][k][:, :, :, None, :])) for k in range(m + 1)]
    er, ei = [q[0] for q in e], [q[1] for q in e]
    fr, fi = [q[0] for q in f], [-q[1] for q in f]
    fexp = [m - 1 - i for i in range(m)]
    bexp = list(range(m))
    b_f = jnp.concatenate([expand_in(er, 0, fexp), expand_in(ei, 0, fexp)], axis=-1)
    b_b = jnp.concatenate([expand_in(er, 1, bexp), expand_in(ei, 1, bexp)], axis=-1)
    c_f = jnp.concatenate([expand_out(fr, 0, [j + 1 for j in range(m)]),
                           expand_out(fi, 0, [j + 1 for j in range(m)])], axis=-2)
    c_b = jnp.concatenate([expand_out(fr, 1, [m - j for j in range(m)]),
                           expand_out(fi, 1, [m - j for j in range(m)])], axis=-2)

    kk = jnp.stack([
        jnp.einsum('ldgop,ldgpn->ldgon', f[k][0], bbar[0], precision=hp)
        - jnp.einsum('ldgop,ldgpn->ldgon', f[k][1], bbar[1], precision=hp)
        for k in range(m)])
    tau = jnp.arange(m)[None, :] - jnp.arange(m)[:, None]
    kf = kk[:, :, 0][jnp.clip(tau, 0, m - 1)]
    kb = kk[:, :, 1][jnp.clip(-tau, 0, m - 1)]
    tmask = lambda c: c[:, :, None, None, None, None]
    toe = jnp.where(tmask(tau >= 0), kf, 0.0) + jnp.where(tmask(tau <= 0), kb, 0.0)
    toe = toe.reshape(m, m, depth, n_oct, OCTET, n, n)
    a_tot = jnp.einsum('ijlogqn,gh->loignjhq', toe, eye, precision=hp)
    a_tot = a_tot.reshape(depth, n_oct, m * OCTET * n, m * OCTET * n)

    lam8 = jnp.concatenate([pw[0][m].reshape(depth, 2, n_oct, 1, OCTET * p),
                            pw[1][m].reshape(depth, 2, n_oct, 1, OCTET * p)], axis=-1)
    lam8 = jnp.moveaxis(lam8, 1, 0)
    return (a_tot.astype(BF16), b_f.astype(BF16), b_b.astype(BF16),
            c_f.astype(BF16), c_b.astype(BF16), lam8)


def _to_chunk_layout(u):
    b, t, w = u.shape
    n_oct = w // LANES
    u = u.reshape(b, t // S5_CHUNK, S5_CHUNK, n_oct, LANES)
    u = jnp.transpose(u, (3, 1, 0, 2, 4))
    return u.reshape(n_oct, (t // S5_CHUNK) * b, S5_CHUNK * LANES)


def _from_chunk_layout(y, b):
    n_oct, rows, _ = y.shape
    mc = rows // b
    y = y.reshape(n_oct, mc, b, S5_CHUNK, LANES)
    y = jnp.transpose(y, (2, 1, 3, 0, 4))
    return y.reshape(b, mc * S5_CHUNK, n_oct * LANES)


def _attn_kernel(*refs, group, hd, chunks):
    qt_ref = refs[0]
    ot_ref = refs[-1]
    kv = refs[1:-1]
    tq = qt_ref.shape[2]
    qt = jnp.concatenate([qt_ref[0, h * hd:(h + 1) * hd, :] for h in range(group)], axis=1)
    cols = group * tq
    m = jnp.full((1, cols), -jnp.inf, F32)
    acc = jnp.zeros((hd + BF16_ROWS, cols), F32)
    for i, tk in enumerate(chunks):
        k_ref, vt_ref = kv[2 * i], kv[2 * i + 1]
        ones = jnp.ones((BF16_ROWS, tk), BF16)
        for c in range(k_ref.shape[1] // tk):
            k = k_ref[0, c * tk:(c + 1) * tk, :]
            vt = jnp.concatenate([vt_ref[0, :, c * tk:(c + 1) * tk], ones], axis=0)
            st = _dot(k, qt)
            m_new = jnp.maximum(m, jnp.max(st, axis=0, keepdims=True))
            alpha = jnp.exp2(m - m_new)
            p = jnp.exp2(st - m_new)
            acc = alpha * acc + _dot(vt, p.astype(BF16))
            m = m_new
    ot = acc[:hd] / acc[hd:hd + 1]
    for h in range(group):
        ot_ref[0, h * hd:(h + 1) * hd, :] = ot[:, h * tq:(h + 1) * tq].astype(BF16)


def _attention(qt, kvs, hd, tq):
    b, aw, t = qt.shape
    group = aw // hd // N_KV_HEADS
    gw = group * hd
    chunks = tuple(min(k.shape[1], 512) for k, _ in kvs)
    in_specs = [pl.BlockSpec((1, gw, tq), lambda bi, j, i: (bi, j, i))]
    args = [qt]
    for k, vt in kvs:
        s = k.shape[1]
        in_specs.append(pl.BlockSpec((1, s, hd), lambda bi, j, i: (bi, 0, j)))
        in_specs.append(pl.BlockSpec((1, hd, s), lambda bi, j, i: (bi, j, 0)))
        args += [k, vt]
    return pl.pallas_call(
        functools.partial(_attn_kernel, group=group, hd=hd, chunks=chunks),
        grid=(b, N_KV_HEADS, t // tq),
        in_specs=in_specs,
        out_specs=pl.BlockSpec((1, gw, tq), lambda bi, j, i: (bi, j, i)),
        out_shape=jax.ShapeDtypeStruct((b, aw, t), BF16),
        compiler_params=_params(("arbitrary", "arbitrary", "arbitrary")),
        name="attention",
    )(*args)


def _outproj_kernel(zc_ref, zp_ref, zn_ref, yf_ref, yb_ref, u_ref, at_ref, x_ref, mod_ref,
                    cwt_ref, d_ref, wglu_ref, bglu_ref, wo_ref, g_ref, o_ref, ubuf_ref, *, cw):
    i = pl.program_id(1)
    tm = x_ref.shape[1]

    def gated(z):
        z = z.astype(F32)
        return z[:, 2 * cw:] * z[:, :cw]

    zc = zc_ref[0].astype(F32)
    u = zc[:, 2 * cw:] * zc[:, :cw]
    prev = gated(zp_ref[0])[BF16_ROWS - 1:BF16_ROWS]
    nxt = gated(zn_ref[0])[0:1]
    ubuf_ref[SUBLANES - 1:SUBLANES, :] = jnp.where(i > 0, prev, 0.0)
    ubuf_ref[SUBLANES:SUBLANES + tm, :] = u
    ubuf_ref[SUBLANES + tm:SUBLANES + tm + 1, :] = jnp.where(i < pl.num_programs(1) - 1, nxt, 0.0)
    y = (cwt_ref[0, 0:1] * ubuf_ref[SUBLANES - 1:SUBLANES - 1 + tm, :] + cwt_ref[0, 1:2] * u
         + cwt_ref[0, 2:3] * ubuf_ref[SUBLANES + 1:SUBLANES + 1 + tm, :])
    conv = (zc[:, cw:2 * cw] * y).astype(BF16)

    ys = yf_ref[0] + yb_ref[0] + d_ref[0] * u_ref[0].astype(F32)
    gl = jax.nn.gelu(ys)
    gate = jax.nn.sigmoid(_dot(gl.astype(BF16), wglu_ref[0]) + bglu_ref[0])
    ssm = (gl * gate).astype(BF16)

    ns = conv.shape[1] + ssm.shape[1]
    tn = (((0,), (0,)), ((), ()))
    mix = (_dot(jnp.concatenate([conv, ssm], axis=1), wo_ref[0, 0:ns, :])
           + lax.dot_general(at_ref[0], wo_ref[0, ns:, :], tn, preferred_element_type=F32))
    o_ref[0] = x_ref[0] + mod_ref[0, 2:3, :] * _rms(mix, g_ref[0])


def _outproj(zc, yf, yb, u, at, x, mod, conv_w, ssm_d, w_glu, b_glu, w_out, gain, layer, dims, tm):
    b, t, d = x.shape
    cw, sw, aw, hd = dims
    nhalo = tm // BF16_ROWS
    last = t // BF16_ROWS - 1
    tok = lambda w: pl.BlockSpec((1, tm, w), lambda bi, i: (bi, i, 0))
    return pl.pallas_call(
        functools.partial(_outproj_kernel, cw=cw),
        grid=(b, t // tm),
        in_specs=[
            tok(3 * cw),
            pl.BlockSpec((1, BF16_ROWS, 3 * cw), lambda bi, i: (bi, jnp.maximum(i * nhalo - 1, 0), 0)),
            pl.BlockSpec((1, BF16_ROWS, 3 * cw), lambda bi, i: (bi, jnp.minimum((i + 1) * nhalo, last), 0)),
            tok(sw), tok(sw), tok(sw),
            pl.BlockSpec((1, aw, tm), lambda bi, i: (bi, 0, i)),
            tok(d),
            pl.BlockSpec((1, MOD_ROWS, d), lambda bi, i: (bi, 0, 0)),
            _layer_spec((conv_w.shape[1], cw), layer),
            _layer_spec((1, sw), layer),
            _layer_spec((sw, sw), layer),
            _layer_spec((1, sw), layer),
            _layer_spec((d, d), layer),
            _layer_spec((1, d), layer),
        ],
        out_specs=tok(d),
        out_shape=jax.ShapeDtypeStruct((b, t, d), F32),
        scratch_shapes=[pltpu.VMEM((tm + 2 * SUBLANES, cw), F32)],
        compiler_params=_params(("arbitrary", "arbitrary")),
        name="outproj",
    )(zc, zc, zc, yf, yb, u, at, x, mod, conv_w, ssm_d, w_glu, b_glu, w_out, gain)


def _ffn_kernel(x_ref, mod_ref, gpre_ref, gpost_ref, wg_ref, wu_ref, wd_ref, o_ref, h_ref, acc_ref):
    j = pl.program_id(2)

    @pl.when(j == 0)
    def _():
        h = _rms(x_ref[0], gpre_ref[0]) * (1.0 + mod_ref[0, 4:5, :]) + mod_ref[0, 3:4, :]
        h_ref[...] = h.astype(BF16)
        acc_ref[...] = jnp.zeros_like(acc_ref)

    h = h_ref[...]
    a = jax.nn.silu(_dot(h, wg_ref[0])) * _dot(h, wu_ref[0])
    acc_ref[...] += _dot(a.astype(BF16), wd_ref[0])

    @pl.when(j == pl.num_programs(2) - 1)
    def _():
        o_ref[0] = x_ref[0] + mod_ref[0, 5:6, :] * _rms(acc_ref[...], gpost_ref[0])


def _ffn(x, mod, g_pre, g_post, w_gate, w_up, w_down, layer, tm, th):
    b, t, d = x.shape
    f = w_gate.shape[-1]
    tok = pl.BlockSpec((1, tm, d), lambda bi, i, j: (bi, i, 0))
    return pl.pallas_call(
        _ffn_kernel,
        grid=(b, t // tm, f // th),
        in_specs=[
            tok,
            pl.BlockSpec((1, MOD_ROWS, d), lambda bi, i, j: (bi, 0, 0)),
            _layer_spec((1, d), layer),
            _layer_spec((1, d), layer),
            pl.BlockSpec((1, d, th), lambda bi, i, j: (layer, 0, j)),
            pl.BlockSpec((1, d, th), lambda bi, i, j: (layer, 0, j)),
            pl.BlockSpec((1, th, d), lambda bi, i, j: (layer, j, 0)),
        ],
        out_specs=tok,
        out_shape=jax.ShapeDtypeStruct((b, t, d), F32),
        scratch_shapes=[pltpu.VMEM((tm, d), BF16), pltpu.VMEM((tm, d), F32)],
        input_output_aliases={0: 0},
        compiler_params=_params(("arbitrary", "arbitrary", "arbitrary")),
        name="ffn",
    )(x, mod, g_pre, g_post, w_gate, w_up, w_down)


def _rope_tables(t, hd):
    rows = t // GRID_W
    row = jnp.broadcast_to(jnp.arange(rows)[:, None], (rows, GRID_W)).reshape(-1).astype(F32)
    col = jnp.broadcast_to(jnp.arange(GRID_W)[None, :], (rows, GRID_W)).reshape(-1).astype(F32)
    half = hd // 2
    inv_freq = ROPE_THETA ** (-jnp.arange(0, half, 2, dtype=F32) / half)
    ar = row[:, None] * inv_freq
    ac = col[:, None] * inv_freq
    cos = jnp.concatenate([jnp.cos(ar), jnp.cos(ar), jnp.cos(ac), jnp.cos(ac)], axis=-1)
    sin = jnp.concatenate([-jnp.sin(ar), jnp.sin(ar), -jnp.sin(ac), jnp.sin(ac)], axis=-1)
    return cos, sin


def _tile(n, pref):
    return pref if n % pref == 0 else n


def kernel(x, c, ctx, c_ctx, w_mod, b_mod, g_pre_mix, g_post_mix, g_pre_ffn, g_post_ffn, w_in, conv_w, ssm_lam_re, ssm_lam_im, ssm_log_dt, ssm_b_re, ssm_b_im, ssm_c_re, ssm_c_im, ssm_d, w_glu, b_glu, q_norm, k_norm, w_out, w_gate, w_up, w_down):
    bsz, t, d = x.shape
    n_ctx = ctx.shape[1]
    depth = w_mod.shape[0]
    cw = conv_w.shape[-1]
    sw = w_glu.shape[-1]
    hd = q_norm.shape[-1]
    aw = d - cw - sw
    dims = (cw, sw, aw, hd)
    assert w_in.shape[-1] == 3 * cw + sw + aw + 2 * N_KV_HEADS * hd
    assert sw % LANES == 0 and t % (S5_CHUNK * BF16_ROWS) == 0 and n_ctx % (S5_CHUNK * BF16_ROWS) == 0

    cvec = jnp.concatenate([c, c_ctx[None, :], jnp.zeros((BF16_ROWS - bsz - 1, d), F32)], axis=0)
    mod = _modulation(cvec, w_mod, b_mod).reshape(depth, BF16_ROWS, N_MOD, d)
    mod = jnp.pad(mod, ((0, 0), (0, 0), (0, MOD_ROWS - N_MOD), (0, 0)))
    mod_x = mod[:, :bsz]
    mod_c = jnp.broadcast_to(mod[:, bsz:bsz + 1], (depth, bsz, MOD_ROWS, d))

    cos, sin = _rope_tables(t, hd)
    tabs = (cos, sin, cos.T, sin.T)
    ops = _s5_operators(ssm_lam_re, ssm_lam_im, ssm_log_dt, ssm_b_re, ssm_b_im, ssm_c_re, ssm_c_im)
    n_oct = sw // LANES

    r3 = lambda a: a.reshape(depth, 1, a.shape[-1])
    g_pre_mix, g_post_mix, g_pre_ffn, g_post_ffn = map(r3, (g_pre_mix, g_post_mix, g_pre_ffn, g_post_ffn))
    q_col = q_norm.reshape(depth, hd, 1)
    k_norm, ssm_d, b_glu = map(r3, (k_norm, ssm_d, b_glu))
    q_off = 3 * cw + sw
    k_off = q_off + aw
    v_off = k_off + N_KV_HEADS * hd
    w_a = jnp.concatenate([w_in[..., :q_off], w_in[..., k_off:v_off]], axis=-1).astype(BF16)
    w_bt = jnp.swapaxes(jnp.concatenate([w_in[..., q_off:k_off], w_in[..., v_off:]], axis=-1), 1, 2).astype(BF16)
    w_glu, w_out, w_gate, w_up, w_down = (w.astype(BF16) for w in (w_glu, w_out, w_gate, w_up, w_down))

    tm_x, tm_c = _tile(t, 512), _tile(n_ctx, 256)
    tq_x, tq_c = _tile(t, 512), _tile(n_ctx, 256)
    th = _tile(w_gate.shape[-1], 512)
    cb_x = _tile(t // S5_CHUNK, 64)
    cb_c = n_ctx // S5_CHUNK
    h_zero = jnp.zeros((2, n_oct, bsz, ops[-1].shape[-1]), F32)

    def s5(zs, h0, layer, cb):
        yf, yb, hfin = _s5(_to_chunk_layout(zs), ops, h0, layer, bsz, cb)
        return _from_chunk_layout(yf, bsz), _from_chunk_layout(yb, bsz), hfin

    xc = ctx
    for l in range(depth):
        want_ctx = l < depth - 1
        zc_x, zs_x, q_x, k_x, v_x = _inproj(x, mod_x[l], g_pre_mix, w_a, w_bt, q_col, k_norm, tabs, l, dims, True, tm_x)
        zc_c, zs_c, q_c, k_c, v_c = _inproj(xc, mod_c[l], g_pre_mix, w_a, w_bt, q_col, k_norm, tabs, l, dims, False, tm_c)
        yf_c, yb_c, h_ctx = s5(zs_c, h_zero, l, cb_c)
        yf_x, yb_x, _ = s5(zs_x, h_ctx, l, cb_x)
        at_x = _attention(q_x, [(k_x, v_x), (k_c, v_c)], hd, tq_x)
        x = _outproj(zc_x, yf_x, yb_x, zs_x, at_x, x, mod_x[l], conv_w, ssm_d, w_glu, b_glu, w_out,
                     g_post_mix, l, dims, tm_x)
        x = _ffn(x, mod_x[l], g_pre_ffn, g_post_ffn, w_gate, w_up, w_down, l, tm_x, th)
        if want_ctx:
            at_c = _attention(q_c, [(k_c, v_c)], hd, tq_c)
            xc = _outproj(zc_c, yf_c, yb_c, zs_c, at_c, xc, mod_c[l], conv_w, ssm_d, w_glu, b_glu, w_out,
                          g_post_mix, l, dims, tm_c)
            xc = _ffn(xc, mod_c[l], g_pre_ffn, g_post_ffn, w_gate, w_up, w_down, l, tm_c, th)
    return x
```

```python
import functools
import math

import jax
import jax.numpy as jnp
from jax import lax
from jax.experimental import pallas as pl
from jax.experimental.pallas import tpu as pltpu

F32 = jnp.float32
BF16 = jnp.bfloat16

GRID_W = 64
N_KV_HEADS = 2
N_MOD = 6
ROPE_THETA = 10000.0
RMS_EPS = 1e-6

LANES = 128
SUBLANES = 8
BF16_ROWS = 16
V7X_VMEM_LIMIT_BYTES = 56 * 1024 * 1024

S5_CHUNK = 8
OCTET = LANES // 16

MOD_ROWS = 8
ROW_CHUNK = 16


def _dot(a, b):
    return jnp.dot(a, b, preferred_element_type=F32)


def _rms(x, gain):
    ms = jnp.mean(x * x, axis=-1, keepdims=True)
    return x * lax.rsqrt(ms + RMS_EPS) * gain


def _const_spec(shape):
    nd = len(shape)
    return pl.BlockSpec(shape, lambda *_: (0,) * nd, pipeline_mode=pl.Buffered(1))


def _layer_spec(shape, layer):
    nd = len(shape)
    return pl.BlockSpec((1,) + shape, lambda *_: (layer,) + (0,) * nd, pipeline_mode=pl.Buffered(1))


def _params(semantics):
    return pltpu.CompilerParams(dimension_semantics=semantics, vmem_limit_bytes=V7X_VMEM_LIMIT_BYTES)


def _mod_kernel(c_ref, w_ref, b_ref, o_ref):
    s = jax.nn.silu(c_ref[...]).astype(BF16)
    o_ref[0] = _dot(s, w_ref[0].astype(BF16)) + b_ref[0]


def _modulation(cvec, w_mod, b_mod):
    depth, d, n = w_mod.shape
    r = cvec.shape[0]
    tn = min(n, 1024)
    return pl.pallas_call(
        _mod_kernel,
        grid=(depth, n // tn),
        in_specs=[
            pl.BlockSpec((r, d), lambda l, j: (0, 0)),
            pl.BlockSpec((1, d, tn), lambda l, j: (l, 0, j)),
            pl.BlockSpec((1, 1, tn), lambda l, j: (l, 0, j)),
        ],
        out_specs=pl.BlockSpec((1, r, tn), lambda l, j: (l, 0, j)),
        out_shape=jax.ShapeDtypeStruct((depth, r, n), F32),
        compiler_params=_params(("arbitrary", "arbitrary")),
        name="modulation",
    )(cvec, w_mod, b_mod.reshape(depth, 1, n))


def _swap32(y, lane_lo):
    return jnp.where(lane_lo, pltpu.roll(y, LANES - 32, 1), pltpu.roll(y, 32, 1))


def _swap32_rows(y):
    return jnp.concatenate([y[32:64], y[0:32], y[96:128], y[64:96]], axis=0)


def _inproj_kernel(x_ref, mod_ref, g_ref, wa_ref, wbt_ref, qg_ref, kg_ref, cos_ref, sin_ref, cost_ref, sint_ref,
                   zc_ref, zs_ref, qt_ref, k_ref, vt_ref, *, cw, sw, aw, hd, rope):
    x = x_ref[0]
    h = _rms(x, g_ref[0] * (1.0 + mod_ref[0, 1:2, :])) + mod_ref[0, 0:1, :]
    h = h.astype(BF16)
    kvw = N_KV_HEADS * hd
    k_off = 3 * cw + sw
    zc_ref[0] = _dot(h, wa_ref[0, :, 0:3 * cw]).astype(BF16)
    zs_ref[0] = _dot(h, wa_ref[0, :, 3 * cw:k_off]).astype(BF16)

    tm = x.shape[0]
    nt = (((1,), (1,)), ((), ()))
    vt_ref[0] = lax.dot_general(wbt_ref[0, aw:aw + kvw, :], h, nt, preferred_element_type=F32).astype(BF16)

    lane_lo = (lax.broadcasted_iota(jnp.int32, (tm, hd), 1) & 32) == 0
    k = _dot(h, wa_ref[0, :, k_off:k_off + kvw])
    for i in range(N_KV_HEADS):
        y = _rms(k[:, i * hd:(i + 1) * hd], kg_ref[0])
        if rope:
            y = y * cos_ref[...] + _swap32(y, lane_lo) * sin_ref[...]
        k_ref[0, :, i * hd:(i + 1) * hd] = y.astype(BF16)

    qscale = hd ** -0.5 * math.log2(math.e)
    qt = lax.dot_general(wbt_ref[0, 0:aw, :], h, nt, preferred_element_type=F32)
    for i in range(aw // hd):
        z = qt[i * hd:(i + 1) * hd, :]
        ms = jnp.mean(z * z, axis=0, keepdims=True)
        y = z * lax.rsqrt(ms + RMS_EPS) * qg_ref[0]
        if rope:
            y = y * cost_ref[...] + _swap32_rows(y) * sint_ref[...]
        qt_ref[0, i * hd:(i + 1) * hd, :] = (y * qscale).astype(BF16)


def _inproj(x, mod, gain, w_a, w_bt, q_gain_col, k_gain, tabs, layer, dims, rope, tm):
    b, t, d = x.shape
    cw, sw, aw, hd = dims
    kvw = N_KV_HEADS * hd
    cos, sin, cos_t, sin_t = tabs
    tok = lambda w: pl.BlockSpec((1, tm, w), lambda bi, i: (bi, i, 0))
    tok_t = lambda w: pl.BlockSpec((1, w, tm), lambda bi, i: (bi, 0, i))
    tab = pl.BlockSpec((tm, hd), lambda bi, i: (i, 0))
    tab_t = pl.BlockSpec((hd, tm), lambda bi, i: (0, i))
    sds = jax.ShapeDtypeStruct
    return pl.pallas_call(
        functools.partial(_inproj_kernel, cw=cw, sw=sw, aw=aw, hd=hd, rope=rope),
        grid=(b, t // tm),
        in_specs=[
            tok(d),
            pl.BlockSpec((1, MOD_ROWS, d), lambda bi, i: (bi, 0, 0)),
            _layer_spec((1, d), layer),
            _layer_spec((d, w_a.shape[-1]), layer),
            _layer_spec((aw + kvw, d), layer),
            _layer_spec((hd, 1), layer),
            _layer_spec((1, hd), layer),
            tab, tab, tab_t, tab_t,
        ],
        out_specs=[tok(3 * cw), tok(sw), tok_t(aw), tok(kvw), tok_t(kvw)],
        out_shape=[sds((b, t, 3 * cw), BF16), sds((b, t, sw), BF16), sds((b, aw, t), BF16),
                   sds((b, t, kvw), BF16), sds((b, kvw, t), BF16)],
        compiler_params=_params(("arbitrary", "arbitrary")),
        name="inproj",
    )(x, mod, gain, w_a, w_bt, q_gain_col, k_gain, cos, sin, cos_t, sin_t)


def _s5_kernel(x_ref, atc_ref, bfc_ref, bbc_ref, cfc_ref, cbc_ref, ra_ref, rb_ref, lam_ref, h0_ref,
               y_ref, hfin_ref,
               at_ref, bf_ref, bb_ref, cf_ref, cb_ref, s_ref, hb_all_ref, h_ref,
               *, cb, nb, nblk, n_shift, p_shift):
    s = pl.program_id(1)
    half = h_ref.shape[-1] // 2
    blk = cb * nb

    def expand(comp_ref, r_ref, row_shift, col_shift):
        full = _dot(comp_ref[0, 0], r_ref[...])
        rg = (lax.broadcasted_iota(jnp.int32, full.shape, 0) >> row_shift) & (OCTET - 1)
        cg = (lax.broadcasted_iota(jnp.int32, full.shape, 1) >> col_shift) & (OCTET - 1)
        return jnp.where(rg == cg, full, 0.0).astype(at_ref.dtype)

    @pl.when(s == 0)
    def _():
        at_ref[...] = expand(atc_ref, ra_ref, n_shift, n_shift)
        bf_ref[...] = expand(bfc_ref, rb_ref, n_shift, p_shift)
        bb_ref[...] = expand(bbc_ref, rb_ref, n_shift, p_shift)
        cf_ref[...] = expand(cfc_ref, ra_ref, p_shift, n_shift)
        cb_ref[...] = expand(cbc_ref, ra_ref, p_shift, n_shift)
        h_ref[...] = h0_ref[1, 0]

    def scan(lam, reverse):
        lr = jnp.broadcast_to(lam[:, :half], (nb, half))
        li = jnp.broadcast_to(lam[:, half:], (nb, half))

        def body(k, carry):
            hr, hi = carry
            c = (cb - 1 - k) if reverse else k
            r0 = pl.multiple_of(c * nb, nb)
            srow = s_ref[pl.ds(r0, nb), :]
            s_ref[pl.ds(r0, nb), :] = jnp.concatenate([hr, hi], axis=1)
            nhr = lr * hr - li * hi + srow[:, :half]
            nhi = lr * hi + li * hr + srow[:, half:]
            return nhr, nhi

        hr, hi = lax.fori_loop(0, cb, body, (h_ref[:, :half], h_ref[:, half:]), unroll=2)
        h_ref[...] = jnp.concatenate([hr, hi], axis=1)

    @pl.when(s < nblk)
    def _():
        s_ref[...] = _dot(x_ref[0], bb_ref[...])
        scan(lam_ref[1, 0, 0], True)
        r0 = pl.multiple_of((nblk - 1 - s) * blk, blk)
        hb_all_ref[pl.ds(r0, blk), :] = s_ref[...].astype(hb_all_ref.dtype)

    @pl.when(s == nblk - 1)
    def _():
        hfin_ref[1, 0] = h_ref[...]
        h_ref[...] = h0_ref[0, 0]

    @pl.when(s >= nblk)
    def _():
        x = x_ref[0]
        s_ref[...] = _dot(x, bf_ref[...])
        scan(lam_ref[0, 0, 0], False)
        r0 = pl.multiple_of((s - nblk) * blk, blk)
        y = (_dot(x, at_ref[...]) + _dot(s_ref[...].astype(cf_ref.dtype), cf_ref[...])
             + _dot(hb_all_ref[pl.ds(r0, blk), :], cb_ref[...]))
        y_ref[0] = y.astype(y_ref.dtype)

    @pl.when(s == 2 * nblk - 1)
    def _():
        hfin_ref[0, 0] = h_ref[...]


def _s5(xq, ops, h0, layer, nb, cb):
    at_c, bf_c, bb_c, cf_c, cb_c, r_a, r_b, lam8, n_shift, p_shift = ops
    n_oct, rows, kx = xq.shape
    sw = lam8.shape[-1]
    blk = cb * nb
    nblk = rows // blk
    xblk = lambda o, s: (o, jnp.where(s < nblk, nblk - 1 - s, s - nblk), 0)
    yblk = lambda o, s: (o, jnp.maximum(s - nblk, 0), 0)
    comp = lambda a: pl.BlockSpec((1, 1) + a.shape[2:], lambda o, s: (layer, o, 0, 0))
    whole = lambda a: pl.BlockSpec(a.shape, lambda o, s: (0, 0))
    dt = xq.dtype
    return pl.pallas_call(
        functools.partial(_s5_kernel, cb=cb, nb=nb, nblk=nblk, n_shift=n_shift, p_shift=p_shift),
        grid=(n_oct, 2 * nblk),
        in_specs=[
            pl.BlockSpec((1, blk, kx), xblk),
            comp(at_c), comp(bf_c), comp(bb_c), comp(cf_c), comp(cb_c), whole(r_a), whole(r_b),
            pl.BlockSpec((2, 1, 1, 1, sw), lambda o, s: (0, layer, o, 0, 0)),
            pl.BlockSpec((2, 1, nb, sw), lambda o, s: (0, o, 0, 0)),
        ],
        out_specs=[
            pl.BlockSpec((1, blk, kx), yblk),
            pl.BlockSpec((2, 1, nb, sw), lambda o, s: (0, o, 0, 0)),
        ],
        out_shape=[
            jax.ShapeDtypeStruct((n_oct, rows, kx), dt),
            jax.ShapeDtypeStruct((2, n_oct, nb, sw), F32),
        ],
        scratch_shapes=[
            pltpu.VMEM((kx, kx), dt), pltpu.VMEM((kx, sw), dt), pltpu.VMEM((kx, sw), dt),
            pltpu.VMEM((sw, kx), dt), pltpu.VMEM((sw, kx), dt),
            pltpu.VMEM((blk, sw), F32), pltpu.VMEM((rows, sw), dt), pltpu.VMEM((nb, sw), F32),
        ],
        compiler_params=_params(("arbitrary", "arbitrary")),
        name="s5_mixer",
    )(xq, at_c, bf_c, bb_c, cf_c, cb_c, r_a, r_b, lam8, h0)


def _cmul(a, b):
    return a[0] * b[0] - a[1] * b[1], a[0] * b[1] + a[1] * b[0]


def _s5_operators(lam_re, lam_im, log_dt, b_re, b_im, c_re, c_im):
    hp = lax.Precision.HIGHEST
    depth, _, g, p = lam_re.shape
    n = b_re.shape[-1]
    n_oct = g // OCTET
    m = S5_CHUNK
    assert n & (n - 1) == 0 and p & (p - 1) == 0 and m * n == LANES and 2 * p == LANES
    dt = jnp.exp(log_dt)[..., None]
    mag = jnp.exp(lam_re * dt)
    lam_bar = (mag * jnp.cos(lam_im * dt), mag * jnp.sin(lam_im * dt))
    den = lam_re * lam_re + lam_im * lam_im
    num = (lam_bar[0] - 1.0, lam_bar[1])
    ratio = ((num[0] * lam_re + num[1] * lam_im) / den, (num[1] * lam_re - num[0] * lam_im) / den)
    bbar = _cmul((ratio[0][..., None], ratio[1][..., None]), (b_re, b_im))
    pows = [(jnp.ones_like(lam_re), jnp.zeros_like(lam_re))]
    for _ in range(m):
        pows.append(_cmul(pows[-1], lam_bar))
    pw = (jnp.stack([q[0] for q in pows]), jnp.stack([q[1] for q in pows]))

    def compact_in(e, d, exps):
        sel = jnp.stack([e[k][:, d] for k in exps]).reshape(m, depth, n_oct, OCTET, p, n)
        return jnp.transpose(sel, (1, 2, 0, 3, 5, 4)).reshape(depth, n_oct, m * OCTET * n, p)

    def compact_out(f, d, exps):
        sel = jnp.stack([f[k][:, d] for k in exps]).reshape(m, depth, n_oct, OCTET, n, p)
        return jnp.transpose(sel, (1, 2, 3, 5, 0, 4)).reshape(depth, n_oct, OCTET * p, m * n)

    e = [_cmul((pw[0][k][..., None], pw[1][k][..., None]), bbar) for k in range(m + 1)]
    f = [_cmul((c_re, c_im), (pw[0][k][:, :, :, None, :], pw[1][k][:, :, :, None, :])) for k in range(m + 1)]
    er, ei = [q[0] for q in e], [q[1] for q in e]
    fr, fi = [q[0] for q in f], [-q[1] for q in f]
    fexp = [m - 1 - i for i in range(m)]
    bexp = list(range(m))
    fout = [j + 1 for j in range(m)]
    bout = [m - j for j in range(m)]
    b_f = jnp.concatenate([compact_in(er, 0, fexp), compact_in(ei, 0, fexp)], axis=-1)
    b_b = jnp.concatenate([compact_in(er, 1, bexp), compact_in(ei, 1, bexp)], axis=-1)
    c_f = jnp.concatenate([compact_out(fr, 0, fout), compact_out(fi, 0, fout)], axis=-2)
    c_b = jnp.concatenate([compact_out(fr, 1, bout), compact_out(fi, 1, bout)], axis=-2)

    kk = jnp.stack([
        jnp.einsum('ldgop,ldgpn->ldgon', f[k][0], bbar[0], precision=hp)
        - jnp.einsum('ldgop,ldgpn->ldgon', f[k][1], bbar[1], precision=hp)
        for k in range(m)])
    tau = jnp.arange(m)[None, :] - jnp.arange(m)[:, None]
    kf = kk[:, :, 0][jnp.clip(tau, 0, m - 1)]
    kb = kk[:, :, 1][jnp.clip(-tau, 0, m - 1)]
    tmask = lambda c: c[:, :, None, None, None, None]
    toe = jnp.where(tmask(tau >= 0), kf, 0.0) + jnp.where(tmask(tau <= 0), kb, 0.0)
    toe = toe.reshape(m, m, depth, n_oct, OCTET, n, n)
    a_tot = jnp.transpose(toe, (2, 3, 0, 4, 6, 1, 5)).reshape(depth, n_oct, m * OCTET * n, m * n)

    col = jnp.arange(OCTET * LANES)
    row = jnp.arange(LANES)
    r_a = (row[:, None] // n == col[None, :] // (OCTET * n)) & (row[:, None] % n == col[None, :] % n)
    r_b = (row[:, None] // p == col[None, :] // (OCTET * p)) & (row[:, None] % p == col[None, :] % p)

    lam8 = jnp.concatenate([pw[0][m].reshape(depth, 2, n_oct, 1, OCTET * p),
                            pw[1][m].reshape(depth, 2, n_oct, 1, OCTET * p)], axis=-1)
    lam8 = jnp.moveaxis(lam8, 1, 0)
    cast = lambda a: a.astype(BF16)
    return (cast(a_tot), cast(b_f), cast(b_b), cast(c_f), cast(c_b), cast(r_a), cast(r_b), lam8,
            n.bit_length() - 1, p.bit_length() - 1)


def _to_chunk_layout(u):
    b, t, w = u.shape
    n_oct = w // LANES
    u = u.reshape(b, t // S5_CHUNK, S5_CHUNK, n_oct, LANES)
    u = jnp.transpose(u, (3, 1, 0, 2, 4))
    return u.reshape(n_oct, (t // S5_CHUNK) * b, S5_CHUNK * LANES)


def _from_chunk_layout(y, b):
    n_oct, rows, _ = y.shape
    mc = rows // b
    y = y.reshape(n_oct, mc, b, S5_CHUNK, LANES)
    y = jnp.transpose(y, (2, 1, 3, 0, 4))
    return y.reshape(b, mc * S5_CHUNK, n_oct * LANES)


def _attn_kernel(*refs, group, hd, chunks):
    qt_ref = refs[0]
    ot_ref = refs[-1]
    kv = refs[1:-1]
    tq = qt_ref.shape[2]
    qt = jnp.concatenate([qt_ref[0, h * hd:(h + 1) * hd, :] for h in range(group)], axis=1)
    cols = group * tq
    m = jnp.full((1, cols), -jnp.inf, F32)
    acc = jnp.zeros((hd + BF16_ROWS, cols), F32)
    for i, tk in enumerate(chunks):
        k_ref, vt_ref = kv[2 * i], kv[2 * i + 1]
        ones = jnp.ones((BF16_ROWS, tk), BF16)
        for c in range(k_ref.shape[1] // tk):
            k = k_ref[0, c * tk:(c + 1) * tk, :]
            vt = jnp.concatenate([vt_ref[0, :, c * tk:(c + 1) * tk], ones], axis=0)
            st = _dot(k, qt)
            m_new = jnp.maximum(m, jnp.max(st, axis=0, keepdims=True))
            alpha = jnp.exp2(m - m_new)
            p = jnp.exp2(st - m_new)
            acc = alpha * acc + _dot(vt, p.astype(BF16))
            m = m_new
    ot = acc[:hd] / acc[hd:hd + 1]
    for h in range(group):
        ot_ref[0, h * hd:(h + 1) * hd, :] = ot[:, h * tq:(h + 1) * tq].astype(BF16)


def _attention(qt, kvs, hd, tq):
    b, aw, t = qt.shape
    group = aw // hd // N_KV_HEADS
    gw = group * hd
    chunks = tuple(min(k.shape[1], 512) for k, _ in kvs)
    in_specs = [pl.BlockSpec((1, gw, tq), lambda bi, j, i: (bi, j, i))]
    args = [qt]
    for k, vt in kvs:
        s = k.shape[1]
        in_specs.append(pl.BlockSpec((1, s, hd), lambda bi, j, i: (bi, 0, j)))
        in_specs.append(pl.BlockSpec((1, hd, s), lambda bi, j, i: (bi, j, 0)))
        args += [k, vt]
    return pl.pallas_call(
        functools.partial(_attn_kernel, group=group, hd=hd, chunks=chunks),
        grid=(b, N_KV_HEADS, t // tq),
        in_specs=in_specs,
        out_specs=pl.BlockSpec((1, gw, tq), lambda bi, j, i: (bi, j, i)),
        out_shape=jax.ShapeDtypeStruct((b, aw, t), BF16),
        compiler_params=_params(("arbitrary", "arbitrary", "arbitrary")),
        name="attention",
    )(*args)


def _outproj_kernel(zc_ref, zp_ref, zn_ref, y_ref, u_ref, at_ref, x_ref, mod_ref,
                    cwt_ref, d_ref, wglu_ref, bglu_ref, wo_ref, g_ref, o_ref, ubuf_ref, *, cw):
    i = pl.program_id(1)
    tm = x_ref.shape[1]

    def gated(z):
        z = z.astype(F32)
        return z[:, 2 * cw:] * z[:, :cw]

    zc = zc_ref[0].astype(F32)
    u = zc[:, 2 * cw:] * zc[:, :cw]
    prev = gated(zp_ref[0])[BF16_ROWS - 1:BF16_ROWS]
    nxt = gated(zn_ref[0])[0:1]
    ubuf_ref[SUBLANES - 1:SUBLANES, :] = jnp.where(i > 0, prev, 0.0)
    ubuf_ref[SUBLANES:SUBLANES + tm, :] = u
    ubuf_ref[SUBLANES + tm:SUBLANES + tm + 1, :] = jnp.where(i < pl.num_programs(1) - 1, nxt, 0.0)
    y = (cwt_ref[0, 0:1] * ubuf_ref[SUBLANES - 1:SUBLANES - 1 + tm, :] + cwt_ref[0, 1:2] * u
         + cwt_ref[0, 2:3] * ubuf_ref[SUBLANES + 1:SUBLANES + 1 + tm, :])
    conv = (zc[:, cw:2 * cw] * y).astype(BF16)

    ys = y_ref[0].astype(F32) + d_ref[0] * u_ref[0].astype(F32)
    gl = jax.nn.gelu(ys)
    gate = jax.nn.sigmoid(_dot(gl.astype(BF16), wglu_ref[0]) + bglu_ref[0])
    ssm = (gl * gate).astype(BF16)

    ns = conv.shape[1] + ssm.shape[1]
    tn = (((0,), (0,)), ((), ()))
    mix = (_dot(jnp.concatenate([conv, ssm], axis=1), wo_ref[0, 0:ns, :])
           + lax.dot_general(at_ref[0], wo_ref[0, ns:, :], tn, preferred_element_type=F32))
    o_ref[0] = x_ref[0] + mod_ref[0, 2:3, :] * _rms(mix, g_ref[0])


def _outproj(zc, y, u, at, x, mod, conv_w, ssm_d, w_glu, b_glu, w_out, gain, layer, dims, tm):
    b, t, d = x.shape
    cw, sw, aw, hd = dims
    nhalo = tm // BF16_ROWS
    last = t // BF16_ROWS - 1
    tok = lambda w: pl.BlockSpec((1, tm, w), lambda bi, i: (bi, i, 0))
    return pl.pallas_call(
        functools.partial(_outproj_kernel, cw=cw),
        grid=(b, t // tm),
        in_specs=[
            tok(3 * cw),
            pl.BlockSpec((1, BF16_ROWS, 3 * cw), lambda bi, i: (bi, jnp.maximum(i * nhalo - 1, 0), 0)),
            pl.BlockSpec((1, BF16_ROWS, 3 * cw), lambda bi, i: (bi, jnp.minimum((i + 1) * nhalo, last), 0)),
            tok(sw), tok(sw),
            pl.BlockSpec((1, aw, tm), lambda bi, i: (bi, 0, i)),
            tok(d),
            pl.BlockSpec((1, MOD_ROWS, d), lambda bi, i: (bi, 0, 0)),
            _layer_spec((conv_w.shape[1], cw), layer),
            _layer_spec((1, sw), layer),
            _layer_spec((sw, sw), layer),
            _layer_spec((1, sw), layer),
            _layer_spec((d, d), layer),
            _layer_spec((1, d), layer),
        ],
        out_specs=tok(d),
        out_shape=jax.ShapeDtypeStruct((b, t, d), F32),
        scratch_shapes=[pltpu.VMEM((tm + 2 * SUBLANES, cw), F32)],
        compiler_params=_params(("arbitrary", "arbitrary")),
        name="outproj",
    )(zc, zc, zc, y, u, at, x, mod, conv_w, ssm_d, w_glu, b_glu, w_out, gain)


def _ffn_kernel(x_ref, mod_ref, gpre_ref, gpost_ref, wg_ref, wu_ref, wd_ref, o_ref, h_ref, acc_ref):
    j = pl.program_id(2)

    tm = x_ref.shape[1]

    def row_chunks(fn):
        def body(r, carry):
            fn(pl.ds(pl.multiple_of(r * ROW_CHUNK, ROW_CHUNK), ROW_CHUNK))
            return carry
        lax.fori_loop(0, tm // ROW_CHUNK, body, 0, unroll=4)

    @pl.when(j == 0)
    def _():
        gain = gpre_ref[0] * (1.0 + mod_ref[0, 4:5, :])
        shift = mod_ref[0, 3:4, :]

        def pre(rows):
            h_ref[rows, :] = (_rms(x_ref[0, rows, :], gain) + shift).astype(BF16)
            acc_ref[rows, :] = jnp.zeros((ROW_CHUNK, acc_ref.shape[1]), F32)
        row_chunks(pre)

    h = h_ref[...]
    a = jax.nn.silu(_dot(h, wg_ref[0])) * _dot(h, wu_ref[0])
    acc_ref[...] += _dot(a.astype(BF16), wd_ref[0])

    @pl.when(j == pl.num_programs(2) - 1)
    def _():
        gate = mod_ref[0, 5:6, :]

        def post(rows):
            o_ref[0, rows, :] = x_ref[0, rows, :] + gate * _rms(acc_ref[rows, :], gpost_ref[0])
        row_chunks(post)


def _ffn(x, mod, g_pre, g_post, w_gate, w_up, w_down, layer, tm, th):
    b, t, d = x.shape
    f = w_gate.shape[-1]
    tok = pl.BlockSpec((1, tm, d), lambda bi, i, j: (bi, i, 0))
    return pl.pallas_call(
        _ffn_kernel,
        grid=(b, t // tm, f // th),
        in_specs=[
            tok,
            pl.BlockSpec((1, MOD_ROWS, d), lambda bi, i, j: (bi, 0, 0)),
            _layer_spec((1, d), layer),
            _layer_spec((1, d), layer),
            pl.BlockSpec((1, d, th), lambda bi, i, j: (layer, 0, j)),
            pl.BlockSpec((1, d, th), lambda bi, i, j: (layer, 0, j)),
            pl.BlockSpec((1, th, d), lambda bi, i, j: (layer, j, 0)),
        ],
        out_specs=tok,
        out_shape=jax.ShapeDtypeStruct((b, t, d), F32),
        scratch_shapes=[pltpu.VMEM((tm, d), BF16), pltpu.VMEM((tm, d), F32)],
        input_output_aliases={0: 0},
        compiler_params=_params(("arbitrary", "arbitrary", "arbitrary")),
        name="ffn",
    )(x, mod, g_pre, g_post, w_gate, w_up, w_down)


def _rope_tables(t, hd):
    rows = t // GRID_W
    row = jnp.broadcast_to(jnp.arange(rows)[:, None], (rows, GRID_W)).reshape(-1).astype(F32)
    col = jnp.broadcast_to(jnp.arange(GRID_W)[None, :], (rows, GRID_W)).reshape(-1).astype(F32)
    half = hd // 2
    inv_freq = ROPE_THETA ** (-jnp.arange(0, half, 2, dtype=F32) / half)
    ar = row[:, None] * inv_freq
    ac = col[:, None] * inv_freq
    cos = jnp.concatenate([jnp.cos(ar), jnp.cos(ar), jnp.cos(ac), jnp.cos(ac)], axis=-1)
    sin = jnp.concatenate([-jnp.sin(ar), jnp.sin(ar), -jnp.sin(ac), jnp.sin(ac)], axis=-1)
    return cos, sin


def _tile(n, pref):
    return pref if n % pref == 0 else n


def kernel(x, c, ctx, c_ctx, w_mod, b_mod, g_pre_mix, g_post_mix, g_pre_ffn, g_post_ffn, w_in, conv_w, ssm_lam_re, ssm_lam_im, ssm_log_dt, ssm_b_re, ssm_b_im, ssm_c_re, ssm_c_im, ssm_d, w_glu, b_glu, q_norm, k_norm, w_out, w_gate, w_up, w_down):
    bsz, t, d = x.shape
    n_ctx = ctx.shape[1]
    depth = w_mod.shape[0]
    cw = conv_w.shape[-1]
    sw = w_glu.shape[-1]
    hd = q_norm.shape[-1]
    aw = d - cw - sw
    dims = (cw, sw, aw, hd)
    assert w_in.shape[-1] == 3 * cw + sw + aw + 2 * N_KV_HEADS * hd
    assert sw % LANES == 0 and t % (S5_CHUNK * BF16_ROWS) == 0 and n_ctx % (S5_CHUNK * BF16_ROWS) == 0

    cvec = jnp.concatenate([c, c_ctx[None, :], jnp.zeros((BF16_ROWS - bsz - 1, d), F32)], axis=0)
    mod = _modulation(cvec, w_mod, b_mod).reshape(depth, BF16_ROWS, N_MOD, d)
    mod = jnp.pad(mod, ((0, 0), (0, 0), (0, MOD_ROWS - N_MOD), (0, 0)))
    mod_x = mod[:, :bsz]
    mod_c = jnp.broadcast_to(mod[:, bsz:bsz + 1], (depth, bsz, MOD_ROWS, d))

    cos, sin = _rope_tables(t, hd)
    tabs = (cos, sin, cos.T, sin.T)
    ops = _s5_operators(ssm_lam_re, ssm_lam_im, ssm_log_dt, ssm_b_re, ssm_b_im, ssm_c_re, ssm_c_im)
    n_oct = sw // LANES

    r3 = lambda a: a.reshape(depth, 1, a.shape[-1])
    g_pre_mix, g_post_mix, g_pre_ffn, g_post_ffn = map(r3, (g_pre_mix, g_post_mix, g_pre_ffn, g_post_ffn))
    q_col = q_norm.reshape(depth, hd, 1)
    k_norm, ssm_d, b_glu = map(r3, (k_norm, ssm_d, b_glu))
    q_off = 3 * cw + sw
    k_off = q_off + aw
    v_off = k_off + N_KV_HEADS * hd
    w_a = jnp.concatenate([w_in[..., :q_off], w_in[..., k_off:v_off]], axis=-1).astype(BF16)
    w_bt = jnp.swapaxes(jnp.concatenate([w_in[..., q_off:k_off], w_in[..., v_off:]], axis=-1), 1, 2).astype(BF16)
    w_glu, w_out, w_gate, w_up, w_down = (w.astype(BF16) for w in (w_glu, w_out, w_gate, w_up, w_down))

    tm_x, tm_c = _tile(t, 512), _tile(n_ctx, 256)
    tq_x, tq_c = _tile(t, 512), _tile(n_ctx, 256)
    th = _tile(w_gate.shape[-1], 512)
    cb_x = _tile(t // S5_CHUNK, 128)
    cb_c = n_ctx // S5_CHUNK
    h_zero = jnp.zeros((2, n_oct, bsz, 2 * OCTET * ssm_lam_re.shape[-1]), F32)

    def s5(zs, h0, layer, cb):
        y, hfin = _s5(_to_chunk_layout(zs), ops, h0, layer, bsz, cb)
        return _from_chunk_layout(y, bsz), hfin

    xc = ctx
    for l in range(depth):
        want_ctx = l < depth - 1
        zc_x, zs_x, q_x, k_x, v_x = _inproj(x, mod_x[l], g_pre_mix, w_a, w_bt, q_col, k_norm, tabs, l, dims, True, tm_x)
        zc_c, zs_c, q_c, k_c, v_c = _inproj(xc, mod_c[l], g_pre_mix, w_a, w_bt, q_col, k_norm, tabs, l, dims, False, tm_c)
        y_c, h_ctx = s5(zs_c, h_zero, l, cb_c)
        y_x, _ = s5(zs_x, h_ctx, l, cb_x)
        at_x = _attention(q_x, [(k_x, v_x), (k_c, v_c)], hd, tq_x)
        x = _outproj(zc_x, y_x, zs_x, at_x, x, mod_x[l], conv_w, ssm_d, w_glu, b_glu, w_out,
                     g_post_mix, l, dims, tm_x)
        x = _ffn(x, mod_x[l], g_pre_ffn, g_post_ffn, w_gate, w_up, w_down, l, tm_x, th)
        if want_ctx:
            at_c = _attention(q_c, [(k_c, v_c)], hd, tq_c)
            xc = _outproj(zc_c, y_c, zs_c, at_c, xc, mod_c[l], conv_w, ssm_d, w_glu, b_glu, w_out,
                          g_post_mix, l, dims, tm_c)
            xc = _ffn(xc, mod_c[l], g_pre_ffn, g_post_ffn, w_gate, w_up, w_down, l, tm_c, th)
    return x
```

```python
import functools
import math

import jax
import jax.numpy as jnp
from jax import lax
from jax.experimental import pallas as pl
from jax.experimental.pallas import tpu as pltpu

F32 = jnp.float32
BF16 = jnp.bfloat16

GRID_W = 64
N_KV_HEADS = 2
N_MOD = 6
ROPE_THETA = 10000.0
RMS_EPS = 1e-6

LANES = 128
SUBLANES = 8
BF16_ROWS = 16
V7X_VMEM_LIMIT_BYTES = 56 * 1024 * 1024

S5_CHUNK = 8
OCTET = LANES // 16

MOD_ROWS = 8
ROW_CHUNK = 16


def _dot(a, b):
    return jnp.dot(a, b, preferred_element_type=F32)


def _rms(x, gain):
    ms = jnp.mean(x * x, axis=-1, keepdims=True)
    return x * lax.rsqrt(ms + RMS_EPS) * gain


def _const_spec(shape):
    nd = len(shape)
    return pl.BlockSpec(shape, lambda *_: (0,) * nd, pipeline_mode=pl.Buffered(1))


def _layer_spec(shape, layer):
    nd = len(shape)
    return pl.BlockSpec((1,) + shape, lambda *_: (layer,) + (0,) * nd, pipeline_mode=pl.Buffered(1))


def _params(semantics):
    return pltpu.CompilerParams(dimension_semantics=semantics, vmem_limit_bytes=V7X_VMEM_LIMIT_BYTES)


def _mod_kernel(c_ref, w_ref, b_ref, o_ref):
    s = jax.nn.silu(c_ref[...]).astype(BF16)
    o_ref[0] = _dot(s, w_ref[0].astype(BF16)) + b_ref[0]


def _modulation(cvec, w_mod, b_mod):
    depth, d, n = w_mod.shape
    r = cvec.shape[0]
    tn = min(n, 1024)
    return pl.pallas_call(
        _mod_kernel,
        grid=(depth, n // tn),
        in_specs=[
            pl.BlockSpec((r, d), lambda l, j: (0, 0)),
            pl.BlockSpec((1, d, tn), lambda l, j: (l, 0, j)),
            pl.BlockSpec((1, 1, tn), lambda l, j: (l, 0, j)),
        ],
        out_specs=pl.BlockSpec((1, r, tn), lambda l, j: (l, 0, j)),
        out_shape=jax.ShapeDtypeStruct((depth, r, n), F32),
        compiler_params=_params(("arbitrary", "arbitrary")),
        name="modulation",
    )(cvec, w_mod, b_mod.reshape(depth, 1, n))


def _swap32(y, lane_lo):
    return jnp.where(lane_lo, pltpu.roll(y, LANES - 32, 1), pltpu.roll(y, 32, 1))


def _swap32_rows(y):
    return jnp.concatenate([y[32:64], y[0:32], y[96:128], y[64:96]], axis=0)


def _inproj_kernel(x_ref, mod_ref, g_ref, wa_ref, wbt_ref, qg_ref, kg_ref, cos_ref, sin_ref, cost_ref, sint_ref,
                   zc_ref, zs_ref, qt_ref, k_ref, vt_ref, *, cw, sw, aw, hd, rope):
    x = x_ref[0]
    h = _rms(x, g_ref[0] * (1.0 + mod_ref[0, 1:2, :])) + mod_ref[0, 0:1, :]
    h = h.astype(BF16)
    kvw = N_KV_HEADS * hd
    k_off = 3 * cw + sw
    zc_ref[0] = _dot(h, wa_ref[0, :, 0:3 * cw]).astype(BF16)
    zs_ref[0] = _dot(h, wa_ref[0, :, 3 * cw:k_off]).astype(BF16)

    tm = x.shape[0]
    nt = (((1,), (1,)), ((), ()))
    vt_ref[0] = lax.dot_general(wbt_ref[0, aw:aw + kvw, :], h, nt, preferred_element_type=F32).astype(BF16)

    lane_lo = (lax.broadcasted_iota(jnp.int32, (tm, hd), 1) & 32) == 0
    k = _dot(h, wa_ref[0, :, k_off + aw:k_off + aw + kvw])
    for i in range(N_KV_HEADS):
        y = _rms(k[:, i * hd:(i + 1) * hd], kg_ref[0])
        if rope:
            y = y * cos_ref[...] + _swap32(y, lane_lo) * sin_ref[...]
        k_ref[0, :, i * hd:(i + 1) * hd] = y.astype(BF16)

    qscale = hd ** -0.5 * math.log2(math.e)
    qt = lax.dot_general(wbt_ref[0, 0:aw, :], h, nt, preferred_element_type=F32)
    for i in range(aw // hd):
        z = qt[i * hd:(i + 1) * hd, :]
        ms = jnp.mean(z * z, axis=0, keepdims=True)
        y = z * lax.rsqrt(ms + RMS_EPS) * qg_ref[0]
        if rope:
            y = y * cost_ref[...] + _swap32_rows(y) * sint_ref[...]
        qt_ref[0, i * hd:(i + 1) * hd, :] = (y * qscale).astype(BF16)


def _inproj(x, mod, gain, w_a, w_bt, q_gain_col, k_gain, tabs, layer, dims, rope, tm):
    b, t, d = x.shape
    cw, sw, aw, hd = dims
    kvw = N_KV_HEADS * hd
    cos, sin, cos_t, sin_t = tabs
    tok = lambda w: pl.BlockSpec((1, tm, w), lambda bi, i: (bi, i, 0))
    tok_t = lambda w: pl.BlockSpec((1, w, tm), lambda bi, i: (bi, 0, i))
    tab = pl.BlockSpec((tm, hd), lambda bi, i: (i, 0))
    tab_t = pl.BlockSpec((hd, tm), lambda bi, i: (0, i))
    sds = jax.ShapeDtypeStruct
    return pl.pallas_call(
        functools.partial(_inproj_kernel, cw=cw, sw=sw, aw=aw, hd=hd, rope=rope),
        grid=(b, t // tm),
        in_specs=[
            tok(d),
            pl.BlockSpec((1, MOD_ROWS, d), lambda bi, i: (bi, 0, 0)),
            _layer_spec((1, d), layer),
            _layer_spec((d, w_a.shape[-1]), layer),
            _layer_spec((aw + kvw, d), layer),
            _layer_spec((hd, 1), layer),
            _layer_spec((1, hd), layer),
            tab, tab, tab_t, tab_t,
        ],
        out_specs=[tok(3 * cw), tok(sw), tok_t(aw), tok(kvw), tok_t(kvw)],
        out_shape=[sds((b, t, 3 * cw), BF16), sds((b, t, sw), BF16), sds((b, aw, t), BF16),
                   sds((b, t, kvw), BF16), sds((b, kvw, t), BF16)],
        compiler_params=_params(("arbitrary", "arbitrary")),
        name="inproj",
    )(x, mod, gain, w_a, w_bt, q_gain_col, k_gain, cos, sin, cos_t, sin_t)


def _s5_kernel(x_ref, atc_ref, bfc_ref, bbc_ref, cfc_ref, cbc_ref, ra_ref, rb_ref, lam_ref, h0_ref,
               y_ref, hfin_ref,
               at_ref, bf_ref, bb_ref, cf_ref, cb_ref, s_ref, hb_all_ref, h_ref,
               *, cb, nb, nblk, n_shift, p_shift):
    s = pl.program_id(1)
    half = h_ref.shape[-1] // 2
    blk = cb * nb

    def expand(comp_ref, r_ref, row_shift, col_shift):
        full = _dot(comp_ref[0, 0], r_ref[...])
        rg = (lax.broadcasted_iota(jnp.int32, full.shape, 0) >> row_shift) & (OCTET - 1)
        cg = (lax.broadcasted_iota(jnp.int32, full.shape, 1) >> col_shift) & (OCTET - 1)
        return jnp.where(rg == cg, full, 0.0).astype(at_ref.dtype)

    @pl.when(s == 0)
    def _():
        at_ref[...] = expand(atc_ref, ra_ref, n_shift, n_shift)
        bf_ref[...] = expand(bfc_ref, rb_ref, n_shift, p_shift)
        bb_ref[...] = expand(bbc_ref, rb_ref, n_shift, p_shift)
        cf_ref[...] = expand(cfc_ref, ra_ref, p_shift, n_shift)
        cb_ref[...] = expand(cbc_ref, ra_ref, p_shift, n_shift)
        h_ref[...] = h0_ref[1, 0]

    def scan(lam, reverse):
        lr = jnp.broadcast_to(lam[:, :half], (nb, half))
        li = jnp.broadcast_to(lam[:, half:], (nb, half))

        def body(k, carry):
            hr, hi = carry
            c = (cb - 1 - k) if reverse else k
            r0 = pl.multiple_of(c * nb, nb)
            srow = s_ref[pl.ds(r0, nb), :]
            s_ref[pl.ds(r0, nb), :] = jnp.concatenate([hr, hi], axis=1)
            nhr = lr * hr - li * hi + srow[:, :half]
            nhi = lr * hi + li * hr + srow[:, half:]
            return nhr, nhi

        hr, hi = lax.fori_loop(0, cb, body, (h_ref[:, :half], h_ref[:, half:]), unroll=2)
        h_ref[...] = jnp.concatenate([hr, hi], axis=1)

    @pl.when(s < nblk)
    def _():
        s_ref[...] = _dot(x_ref[0], bb_ref[...])
        scan(lam_ref[1, 0, 0], True)
        r0 = pl.multiple_of((nblk - 1 - s) * blk, blk)
        hb_all_ref[pl.ds(r0, blk), :] = s_ref[...].astype(hb_all_ref.dtype)

    @pl.when(s == nblk - 1)
    def _():
        hfin_ref[1, 0] = h_ref[...]
        h_ref[...] = h0_ref[0, 0]

    @pl.when(s >= nblk)
    def _():
        x = x_ref[0]
        s_ref[...] = _dot(x, bf_ref[...])
        scan(lam_ref[0, 0, 0], False)
        r0 = pl.multiple_of((s - nblk) * blk, blk)
        y = (_dot(x, at_ref[...]) + _dot(s_ref[...].astype(cf_ref.dtype), cf_ref[...])
             + _dot(hb_all_ref[pl.ds(r0, blk), :], cb_ref[...]))
        y_ref[0] = y.astype(y_ref.dtype)

    @pl.when(s == 2 * nblk - 1)
    def _():
        hfin_ref[0, 0] = h_ref[...]


def _s5(xq, ops, h0, layer, nb, cb):
    at_c, bf_c, bb_c, cf_c, cb_c, r_a, r_b, lam8, n_shift, p_shift = ops
    n_oct, rows, kx = xq.shape
    sw = lam8.shape[-1]
    blk = cb * nb
    nblk = rows // blk
    xblk = lambda o, s: (o, jnp.where(s < nblk, nblk - 1 - s, s - nblk), 0)
    yblk = lambda o, s: (o, jnp.maximum(s - nblk, 0), 0)
    comp = lambda a: pl.BlockSpec((1, 1) + a.shape[2:], lambda o, s: (layer, o, 0, 0))
    whole = lambda a: pl.BlockSpec(a.shape, lambda o, s: (0, 0))
    dt = xq.dtype
    return pl.pallas_call(
        functools.partial(_s5_kernel, cb=cb, nb=nb, nblk=nblk, n_shift=n_shift, p_shift=p_shift),
        grid=(n_oct, 2 * nblk),
        in_specs=[
            pl.BlockSpec((1, blk, kx), xblk),
            comp(at_c), comp(bf_c), comp(bb_c), comp(cf_c), comp(cb_c), whole(r_a), whole(r_b),
            pl.BlockSpec((2, 1, 1, 1, sw), lambda o, s: (0, layer, o, 0, 0)),
            pl.BlockSpec((2, 1, nb, sw), lambda o, s: (0, o, 0, 0)),
        ],
        out_specs=[
            pl.BlockSpec((1, blk, kx), yblk),
            pl.BlockSpec((2, 1, nb, sw), lambda o, s: (0, o, 0, 0)),
        ],
        out_shape=[
            jax.ShapeDtypeStruct((n_oct, rows, kx), dt),
            jax.ShapeDtypeStruct((2, n_oct, nb, sw), F32),
        ],
        scratch_shapes=[
            pltpu.VMEM((kx, kx), dt), pltpu.VMEM((kx, sw), dt), pltpu.VMEM((kx, sw), dt),
            pltpu.VMEM((sw, kx), dt), pltpu.VMEM((sw, kx), dt),
            pltpu.VMEM((blk, sw), F32), pltpu.VMEM((rows, sw), dt), pltpu.VMEM((nb, sw), F32),
        ],
        compiler_params=_params(("arbitrary", "arbitrary")),
        name="s5_mixer",
    )(xq, at_c, bf_c, bb_c, cf_c, cb_c, r_a, r_b, lam8, h0)


def _cmul(a, b):
    return a[0] * b[0] - a[1] * b[1], a[0] * b[1] + a[1] * b[0]


def _s5_operators(lam_re, lam_im, log_dt, b_re, b_im, c_re, c_im):
    hp = lax.Precision.HIGHEST
    depth, _, g, p = lam_re.shape
    n = b_re.shape[-1]
    n_oct = g // OCTET
    m = S5_CHUNK
    assert n & (n - 1) == 0 and p & (p - 1) == 0 and m * n == LANES and 2 * p == LANES
    dt = jnp.exp(log_dt)[..., None]
    mag = jnp.exp(lam_re * dt)
    lam_bar = (mag * jnp.cos(lam_im * dt), mag * jnp.sin(lam_im * dt))
    den = lam_re * lam_re + lam_im * lam_im
    num = (lam_bar[0] - 1.0, lam_bar[1])
    ratio = ((num[0] * lam_re + num[1] * lam_im) / den, (num[1] * lam_re - num[0] * lam_im) / den)
    bbar = _cmul((ratio[0][..., None], ratio[1][..., None]), (b_re, b_im))
    pows = [(jnp.ones_like(lam_re), jnp.zeros_like(lam_re))]
    for _ in range(m):
        pows.append(_cmul(pows[-1], lam_bar))
    pw = (jnp.stack([q[0] for q in pows]), jnp.stack([q[1] for q in pows]))

    def compact_in(e, d, exps):
        sel = jnp.stack([e[k][:, d] for k in exps]).reshape(m, depth, n_oct, OCTET, p, n)
        return jnp.transpose(sel, (1, 2, 0, 3, 5, 4)).reshape(depth, n_oct, m * OCTET * n, p)

    def compact_out(f, d, exps):
        sel = jnp.stack([f[k][:, d] for k in exps]).reshape(m, depth, n_oct, OCTET, n, p)
        return jnp.transpose(sel, (1, 2, 3, 5, 0, 4)).reshape(depth, n_oct, OCTET * p, m * n)

    e = [_cmul((pw[0][k][..., None], pw[1][k][..., None]), bbar) for k in range(m + 1)]
    f = [_cmul((c_re, c_im), (pw[0][k][:, :, :, None, :], pw[1][k][:, :, :, None, :])) for k in range(m + 1)]
    er, ei = [q[0] for q in e], [q[1] for q in e]
    fr, fi = [q[0] for q in f], [-q[1] for q in f]
    fexp = [m - 1 - i for i in range(m)]
    bexp = list(range(m))
    fout = [j + 1 for j in range(m)]
    bout = [m - j for j in range(m)]
    b_f = jnp.concatenate([compact_in(er, 0, fexp), compact_in(ei, 0, fexp)], axis=-1)
    b_b = jnp.concatenate([compact_in(er, 1, bexp), compact_in(ei, 1, bexp)], axis=-1)
    c_f = jnp.concatenate([compact_out(fr, 0, fout), compact_out(fi, 0, fout)], axis=-2)
    c_b = jnp.concatenate([compact_out(fr, 1, bout), compact_out(fi, 1, bout)], axis=-2)

    kk = jnp.stack([
        jnp.einsum('ldgop,ldgpn->ldgon', f[k][0], bbar[0], precision=hp)
        - jnp.einsum('ldgop,ldgpn->ldgon', f[k][1], bbar[1], precision=hp)
        for k in range(m)])
    tau = jnp.arange(m)[None, :] - jnp.arange(m)[:, None]
    kf = kk[:, :, 0][jnp.clip(tau, 0, m - 1)]
    kb = kk[:, :, 1][jnp.clip(-tau, 0, m - 1)]
    tmask = lambda c: c[:, :, None, None, None, None]
    toe = jnp.where(tmask(tau >= 0), kf, 0.0) + jnp.where(tmask(tau <= 0), kb, 0.0)
    toe = toe.reshape(m, m, depth, n_oct, OCTET, n, n)
    a_tot = jnp.transpose(toe, (2, 3, 0, 4, 6, 1, 5)).reshape(depth, n_oct, m * OCTET * n, m * n)

    col = jnp.arange(OCTET * LANES)
    row = jnp.arange(LANES)
    r_a = (row[:, None] // n == col[None, :] // (OCTET * n)) & (row[:, None] % n == col[None, :] % n)
    r_b = (row[:, None] // p == col[None, :] // (OCTET * p)) & (row[:, None] % p == col[None, :] % p)

    lam8 = jnp.concatenate([pw[0][m].reshape(depth, 2, n_oct, 1, OCTET * p),
                            pw[1][m].reshape(depth, 2, n_oct, 1, OCTET * p)], axis=-1)
    lam8 = jnp.moveaxis(lam8, 1, 0)
    cast = lambda a: a.astype(BF16)
    return (cast(a_tot), cast(b_f), cast(b_b), cast(c_f), cast(c_b), cast(r_a), cast(r_b), lam8,
            n.bit_length() - 1, p.bit_length() - 1)


def _to_chunk_layout(u):
    b, t, w = u.shape
    n_oct = w // LANES
    u = u.reshape(b, t // S5_CHUNK, S5_CHUNK, n_oct, LANES)
    u = jnp.transpose(u, (3, 1, 0, 2, 4))
    return u.reshape(n_oct, (t // S5_CHUNK) * b, S5_CHUNK * LANES)


def _from_chunk_layout(y, b):
    n_oct, rows, _ = y.shape
    mc = rows // b
    y = y.reshape(n_oct, mc, b, S5_CHUNK, LANES)
    y = jnp.transpose(y, (2, 1, 3, 0, 4))
    return y.reshape(b, mc * S5_CHUNK, n_oct * LANES)


def _attn_kernel(*refs, group, hd, chunks):
    qt_ref = refs[0]
    ot_ref = refs[-1]
    kv = refs[1:-1]
    tq = qt_ref.shape[2]
    qt = jnp.concatenate([qt_ref[0, h * hd:(h + 1) * hd, :] for h in range(group)], axis=1)
    cols = group * tq
    m = jnp.full((1, cols), -jnp.inf, F32)
    acc = jnp.zeros((hd + BF16_ROWS, cols), F32)
    for i, tk in enumerate(chunks):
        k_ref, vt_ref = kv[2 * i], kv[2 * i + 1]
        ones = jnp.ones((BF16_ROWS, tk), BF16)
        for c in range(k_ref.shape[1] // tk):
            k = k_ref[0, c * tk:(c + 1) * tk, :]
            vt = jnp.concatenate([vt_ref[0, :, c * tk:(c + 1) * tk], ones], axis=0)
            st = _dot(k, qt)
            m_new = jnp.maximum(m, jnp.max(st, axis=0, keepdims=True))
            alpha = jnp.exp2(m - m_new)
            p = jnp.exp2(st - m_new)
            acc = alpha * acc + _dot(vt, p.astype(BF16))
            m = m_new
    ot = acc[:hd] / acc[hd:hd + 1]
    for h in range(group):
        ot_ref[0, h * hd:(h + 1) * hd, :] = ot[:, h * tq:(h + 1) * tq].astype(BF16)


def _attention(qt, kvs, hd, tq):
    b, aw, t = qt.shape
    group = aw // hd // N_KV_HEADS
    gw = group * hd
    chunks = tuple(min(k.shape[1], 512) for k, _ in kvs)
    in_specs = [pl.BlockSpec((1, gw, tq), lambda bi, j, i: (bi, j, i))]
    args = [qt]
    for k, vt in kvs:
        s = k.shape[1]
        in_specs.append(pl.BlockSpec((1, s, hd), lambda bi, j, i: (bi, 0, j)))
        in_specs.append(pl.BlockSpec((1, hd, s), lambda bi, j, i: (bi, j, 0)))
        args += [k, vt]
    return pl.pallas_call(
        functools.partial(_attn_kernel, group=group, hd=hd, chunks=chunks),
        grid=(b, N_KV_HEADS, t // tq),
        in_specs=in_specs,
        out_specs=pl.BlockSpec((1, gw, tq), lambda bi, j, i: (bi, j, i)),
        out_shape=jax.ShapeDtypeStruct((b, aw, t), BF16),
        compiler_params=_params(("arbitrary", "arbitrary", "arbitrary")),
        name="attention",
    )(*args)


def _outproj_kernel(zc_ref, zp_ref, zn_ref, y_ref, u_ref, at_ref, x_ref, mod_ref,
                    cwt_ref, d_ref, wglu_ref, bglu_ref, wo_ref, g_ref, o_ref, ubuf_ref, *, cw):
    i = pl.program_id(1)
    tm = x_ref.shape[1]

    def gated(z):
        z = z.astype(F32)
        return z[:, 2 * cw:] * z[:, :cw]

    zc = zc_ref[0].astype(F32)
    u = zc[:, 2 * cw:] * zc[:, :cw]
    prev = gated(zp_ref[0])[BF16_ROWS - 1:BF16_ROWS]
    nxt = gated(zn_ref[0])[0:1]
    ubuf_ref[SUBLANES - 1:SUBLANES, :] = jnp.where(i > 0, prev, 0.0)
    ubuf_ref[SUBLANES:SUBLANES + tm, :] = u
    ubuf_ref[SUBLANES + tm:SUBLANES + tm + 1, :] = jnp.where(i < pl.num_programs(1) - 1, nxt, 0.0)
    y = (cwt_ref[0, 0:1] * ubuf_ref[SUBLANES - 1:SUBLANES - 1 + tm, :] + cwt_ref[0, 1:2] * u
         + cwt_ref[0, 2:3] * ubuf_ref[SUBLANES + 1:SUBLANES + 1 + tm, :])
    conv = (zc[:, cw:2 * cw] * y).astype(BF16)

    ys = y_ref[0].astype(F32) + d_ref[0] * u_ref[0].astype(F32)
    gl = jax.nn.gelu(ys)
    gate = jax.nn.sigmoid(_dot(gl.astype(BF16), wglu_ref[0]) + bglu_ref[0])
    ssm = (gl * gate).astype(BF16)

    ns = conv.shape[1] + ssm.shape[1]
    tn = (((0,), (0,)), ((), ()))
    mix = (_dot(jnp.concatenate([conv, ssm], axis=1), wo_ref[0, 0:ns, :])
           + lax.dot_general(at_ref[0], wo_ref[0, ns:, :], tn, preferred_element_type=F32))
    o_ref[0] = x_ref[0] + mod_ref[0, 2:3, :] * _rms(mix, g_ref[0])


def _outproj(zc, y, u, at, x, mod, conv_w, ssm_d, w_glu, b_glu, w_out, gain, layer, dims, tm):
    b, t, d = x.shape
    cw, sw, aw, hd = dims
    nhalo = tm // BF16_ROWS
    last = t // BF16_ROWS - 1
    tok = lambda w: pl.BlockSpec((1, tm, w), lambda bi, i: (bi, i, 0))
    return pl.pallas_call(
        functools.partial(_outproj_kernel, cw=cw),
        grid=(b, t // tm),
        in_specs=[
            tok(3 * cw),
            pl.BlockSpec((1, BF16_ROWS, 3 * cw), lambda bi, i: (bi, jnp.maximum(i * nhalo - 1, 0), 0)),
            pl.BlockSpec((1, BF16_ROWS, 3 * cw), lambda bi, i: (bi, jnp.minimum((i + 1) * nhalo, last), 0)),
            tok(sw), tok(sw),
            pl.BlockSpec((1, aw, tm), lambda bi, i: (bi, 0, i)),
            tok(d),
            pl.BlockSpec((1, MOD_ROWS, d), lambda bi, i: (bi, 0, 0)),
            _layer_spec((conv_w.shape[1], cw), layer),
            _layer_spec((1, sw), layer),
            _layer_spec((sw, sw), layer),
            _layer_spec((1, sw), layer),
            _layer_spec((d, d), layer),
            _layer_spec((1, d), layer),
        ],
        out_specs=tok(d),
        out_shape=jax.ShapeDtypeStruct((b, t, d), F32),
        scratch_shapes=[pltpu.VMEM((tm + 2 * SUBLANES, cw), F32)],
        compiler_params=_params(("arbitrary", "arbitrary")),
        name="outproj",
    )(zc, zc, zc, y, u, at, x, mod, conv_w, ssm_d, w_glu, b_glu, w_out, gain)


def _ffn_kernel(xp_ref, xn_ref, modp_ref, modn_ref, gpre_ref, gpost_ref, wg_ref, wu_ref, wd_ref,
                o_ref, h0_ref, h1_ref, acc0_ref, acc1_ref, *, shares, ntiles):
    r = pl.program_id(0)
    j = pl.program_id(1)
    tm = xp_ref.shape[1]
    share = tm // shares
    base = jnp.minimum(j, shares - 1) * share

    def norm_rows(src_ref, m_ref, dst_ref, rows):
        gain = gpre_ref[0] * (1.0 + m_ref[0, 4:5, :])
        dst_ref[rows, :] = (_rms(src_ref[0, rows, :], gain) + m_ref[0, 3:4, :]).astype(BF16)

    def finish_rows(acc_ref, rows):
        o_ref[0, rows, :] = xp_ref[0, rows, :] + modp_ref[0, 5:6, :] * _rms(acc_ref[rows, :], gpost_ref[0])

    def share_rows():
        return [pl.ds(pl.multiple_of(base + c * ROW_CHUNK, ROW_CHUNK), ROW_CHUNK)
                for c in range(share // ROW_CHUNK)]

    @pl.when((r == 0) & (j == 0))
    def _():
        def first(c, carry):
            rows = pl.ds(pl.multiple_of(c * ROW_CHUNK, ROW_CHUNK), ROW_CHUNK)
            norm_rows(xp_ref, modp_ref, h0_ref, rows)
            acc1_ref[rows, :] = jnp.zeros((ROW_CHUNK, acc1_ref.shape[1]), F32)
            return carry
        lax.fori_loop(0, tm // ROW_CHUNK, first, 0, unroll=4)

    def step(h_ref, hn_ref, acc_ref, accp_ref, first):
        h = h_ref[...]
        a = jax.nn.silu(_dot(h, wg_ref[0])) * _dot(h, wu_ref[0])
        down = _dot(a.astype(BF16), wd_ref[0])
        if first:
            acc_ref[...] = down
        else:
            acc_ref[...] += down
        for rows in share_rows():
            norm_rows(xn_ref, modn_ref, hn_ref, rows)
            finish_rows(accp_ref, rows)

    def drain(accp_ref):
        for rows in share_rows():
            finish_rows(accp_ref, rows)

    even = lax.rem(r, 2) == 0
    for first in (True, False):
        at_j = (j == 0) if first else (j > 0)
        pl.when((r < ntiles) & even & at_j)(
            functools.partial(step, h0_ref, h1_ref, acc0_ref, acc1_ref, first))
        pl.when((r < ntiles) & jnp.logical_not(even) & at_j)(
            functools.partial(step, h1_ref, h0_ref, acc1_ref, acc0_ref, first))
    pl.when(r == ntiles)(functools.partial(drain, acc0_ref if ntiles % 2 else acc1_ref))


def _ffn(x, mod, g_pre, g_post, w_gate, w_up, w_down, layer, tm, th):
    b, t, d = x.shape
    f = w_gate.shape[-1]
    nt = t // tm
    ntiles = b * nt
    nj = f // th
    shares = max(c for c in (8, 4, 2, 1) if c <= nj and (tm // ROW_CHUNK) % c == 0)
    tile = lambda q: (q // nt, q % nt, 0)
    prev = lambda r, j: tile(jnp.maximum(r - 1, 0))
    nxt = lambda r, j: tile(jnp.minimum(r + 1, ntiles - 1))
    hid = lambda r, j: jnp.where(r < ntiles, j, nj - 1)
    return pl.pallas_call(
        functools.partial(_ffn_kernel, shares=shares, ntiles=ntiles),
        grid=(ntiles + 1, nj),
        in_specs=[
            pl.BlockSpec((1, tm, d), prev),
            pl.BlockSpec((1, tm, d), nxt),
            pl.BlockSpec((1, MOD_ROWS, d), lambda r, j: (jnp.maximum(r - 1, 0) // nt, 0, 0)),
            pl.BlockSpec((1, MOD_ROWS, d), lambda r, j: (jnp.minimum(r + 1, ntiles - 1) // nt, 0, 0)),
            _layer_spec((1, d), layer),
            _layer_spec((1, d), layer),
            pl.BlockSpec((1, d, th), lambda r, j: (layer, 0, hid(r, j))),
            pl.BlockSpec((1, d, th), lambda r, j: (layer, 0, hid(r, j))),
            pl.BlockSpec((1, th, d), lambda r, j: (layer, hid(r, j), 0)),
        ],
        out_specs=pl.BlockSpec((1, tm, d), prev),
        out_shape=jax.ShapeDtypeStruct((b, t, d), F32),
        scratch_shapes=[pltpu.VMEM((tm, d), BF16), pltpu.VMEM((tm, d), BF16),
                        pltpu.VMEM((tm, d), F32), pltpu.VMEM((tm, d), F32)],
        compiler_params=_params(("arbitrary", "arbitrary")),
        name="ffn",
    )(x, x, mod, mod, g_pre, g_post, w_gate, w_up, w_down)


def _rope_tables(t, hd):
    rows = t // GRID_W
    row = jnp.broadcast_to(jnp.arange(rows)[:, None], (rows, GRID_W)).reshape(-1).astype(F32)
    col = jnp.broadcast_to(jnp.arange(GRID_W)[None, :], (rows, GRID_W)).reshape(-1).astype(F32)
    half = hd // 2
    inv_freq = ROPE_THETA ** (-jnp.arange(0, half, 2, dtype=F32) / half)
    ar = row[:, None] * inv_freq
    ac = col[:, None] * inv_freq
    cos = jnp.concatenate([jnp.cos(ar), jnp.cos(ar), jnp.cos(ac), jnp.cos(ac)], axis=-1)
    sin = jnp.concatenate([-jnp.sin(ar), jnp.sin(ar), -jnp.sin(ac), jnp.sin(ac)], axis=-1)
    return cos, sin


def _tile(n, pref):
    return pref if n % pref == 0 else n


def kernel(x, c, ctx, c_ctx, w_mod, b_mod, g_pre_mix, g_post_mix, g_pre_ffn, g_post_ffn, w_in, conv_w, ssm_lam_re, ssm_lam_im, ssm_log_dt, ssm_b_re, ssm_b_im, ssm_c_re, ssm_c_im, ssm_d, w_glu, b_glu, q_norm, k_norm, w_out, w_gate, w_up, w_down):
    bsz, t, d = x.shape
    n_ctx = ctx.shape[1]
    depth = w_mod.shape[0]
    cw = conv_w.shape[-1]
    sw = w_glu.shape[-1]
    hd = q_norm.shape[-1]
    aw = d - cw - sw
    dims = (cw, sw, aw, hd)
    assert w_in.shape[-1] == 3 * cw + sw + aw + 2 * N_KV_HEADS * hd
    assert sw % LANES == 0 and t % (S5_CHUNK * BF16_ROWS) == 0 and n_ctx % (S5_CHUNK * BF16_ROWS) == 0

    cvec = jnp.concatenate([c, c_ctx[None, :], jnp.zeros((BF16_ROWS - bsz - 1, d), F32)], axis=0)
    mod = _modulation(cvec, w_mod, b_mod).reshape(depth, BF16_ROWS, N_MOD, d)
    mod = jnp.pad(mod, ((0, 0), (0, 0), (0, MOD_ROWS - N_MOD), (0, 0)))
    mod_x = mod[:, :bsz]
    mod_c = jnp.broadcast_to(mod[:, bsz:bsz + 1], (depth, bsz, MOD_ROWS, d))

    cos, sin = _rope_tables(t, hd)
    tabs = (cos, sin, cos.T, sin.T)
    ops = _s5_operators(ssm_lam_re, ssm_lam_im, ssm_log_dt, ssm_b_re, ssm_b_im, ssm_c_re, ssm_c_im)
    n_oct = sw // LANES

    r3 = lambda a: a.reshape(depth, 1, a.shape[-1])
    g_pre_mix, g_post_mix, g_pre_ffn, g_post_ffn = map(r3, (g_pre_mix, g_post_mix, g_pre_ffn, g_post_ffn))
    q_col = q_norm.reshape(depth, hd, 1)
    k_norm, ssm_d, b_glu = map(r3, (k_norm, ssm_d, b_glu))
    q_off = 3 * cw + sw
    k_off = q_off + aw
    v_off = k_off + N_KV_HEADS * hd
    w_a = w_in.astype(BF16)
    w_bt = jnp.swapaxes(jnp.concatenate([w_a[..., q_off:k_off], w_a[..., v_off:]], axis=-1), 1, 2)
    w_glu, w_out, w_gate, w_up, w_down = (w.astype(BF16) for w in (w_glu, w_out, w_gate, w_up, w_down))

    tm_x, tm_c = _tile(t, 512), _tile(n_ctx, 256)
    tq_x, tq_c = _tile(t, 512), _tile(n_ctx, 256)
    f = w_gate.shape[-1]
    th = max(c for c in range(LANES, 512 + 1, LANES) if f % c == 0)
    cb_x = _tile(t // S5_CHUNK, 128)
    cb_c = n_ctx // S5_CHUNK
    h_zero = jnp.zeros((2, n_oct, bsz, 2 * OCTET * ssm_lam_re.shape[-1]), F32)

    def s5(zs, h0, layer, cb):
        y, hfin = _s5(_to_chunk_layout(zs), ops, h0, layer, bsz, cb)
        return _from_chunk_layout(y, bsz), hfin

    xc = ctx
    for l in range(depth):
        want_ctx = l < depth - 1
        zc_x, zs_x, q_x, k_x, v_x = _inproj(x, mod_x[l], g_pre_mix, w_a, w_bt, q_col, k_norm, tabs, l, dims, True, tm_x)
        zc_c, zs_c, q_c, k_c, v_c = _inproj(xc, mod_c[l], g_pre_mix, w_a, w_bt, q_col, k_norm, tabs, l, dims, False, tm_c)
        y_c, h_ctx = s5(zs_c, h_zero, l, cb_c)
        y_x, _ = s5(zs_x, h_ctx, l, cb_x)
        at_x = _attention(q_x, [(k_x, v_x), (k_c, v_c)], hd, tq_x)
        x = _outproj(zc_x, y_x, zs_x, at_x, x, mod_x[l], conv_w, ssm_d, w_glu, b_glu, w_out,
                     g_post_mix, l, dims, tm_x)
        x = _ffn(x, mod_x[l], g_pre_ffn, g_post_ffn, w_gate, w_up, w_down, l, tm_x, th)
        if want_ctx:
            at_c = _attention(q_c, [(k_c, v_c)], hd, tq_c)
            xc = _outproj(zc_c, y_c, zs_c, at_c, xc, mod_c[l], conv_w, ssm_d, w_glu, b_glu, w_out,
                          g_post_mix, l, dims, tm_c)
            xc = _ffn(xc, mod_c[l], g_pre_ffn, g_post_ffn, w_gate, w_up, w_down, l, tm_c, th)
    return x
```

```python
import functools
import math

import jax
import jax.numpy as jnp
from jax import lax
from jax.experimental import pallas as pl
from jax.experimental.pallas import tpu as pltpu

F32 = jnp.float32
BF16 = jnp.bfloat16

GRID_W = 64
N_KV_HEADS = 2
N_MOD = 6
ROPE_THETA = 10000.0
RMS_EPS = 1e-6

LANES = 128
SUBLANES = 8
BF16_ROWS = 16
V7X_VMEM_LIMIT_BYTES = 56 * 1024 * 1024

S5_CHUNK = 8
OCTET = LANES // 16

MOD_ROWS = 8

def _dot(a, b):
    return jnp.dot(a, b, preferred_element_type=F32)


def _rms(x, gain):
    ms = jnp.mean(x * x, axis=-1, keepdims=True)
    return x * lax.rsqrt(ms + RMS_EPS) * gain


def _const_spec(shape):
    nd = len(shape)
    return pl.BlockSpec(shape, lambda *_: (0,) * nd, pipeline_mode=pl.Buffered(1))


def _layer_spec(shape, layer):
    nd = len(shape)
    return pl.BlockSpec((1,) + shape, lambda *_: (layer,) + (0,) * nd, pipeline_mode=pl.Buffered(1))


def _params(semantics):
    return pltpu.CompilerParams(dimension_semantics=semantics, vmem_limit_bytes=V7X_VMEM_LIMIT_BYTES)


def _mod_kernel(c_ref, w_ref, b_ref, o_ref):
    s = jax.nn.silu(c_ref[...]).astype(BF16)
    o_ref[0] = _dot(s, w_ref[0].astype(BF16)) + b_ref[0]


def _modulation(cvec, w_mod, b_mod):
    depth, d, n = w_mod.shape
    r = cvec.shape[0]
    tn = min(n, 1024)
    return pl.pallas_call(
        _mod_kernel,
        grid=(depth, n // tn),
        in_specs=[
            pl.BlockSpec((r, d), lambda l, j: (0, 0)),
            pl.BlockSpec((1, d, tn), lambda l, j: (l, 0, j)),
            pl.BlockSpec((1, 1, tn), lambda l, j: (l, 0, j)),
        ],
        out_specs=pl.BlockSpec((1, r, tn), lambda l, j: (l, 0, j)),
        out_shape=jax.ShapeDtypeStruct((depth, r, n), F32),
        compiler_params=_params(("arbitrary", "arbitrary")),
        name="modulation",
    )(cvec, w_mod, b_mod.reshape(depth, 1, n))


def _swap32(y, lane_lo):
    return jnp.where(lane_lo, pltpu.roll(y, LANES - 32, 1), pltpu.roll(y, 32, 1))


def _swap32_rows(y):
    return jnp.concatenate([y[32:64], y[0:32], y[96:128], y[64:96]], axis=0)


def _inproj_kernel(x_ref, mod_ref, g_ref, wa_ref, wbt_ref, qg_ref, kg_ref, cos_ref, sin_ref, cost_ref, sint_ref,
                   zc_ref, zs_ref, qt_ref, k_ref, vt_ref, *, cw, sw, aw, hd, rope):
    x = x_ref[0]
    h = _rms(x, g_ref[0] * (1.0 + mod_ref[0, 1:2, :])) + mod_ref[0, 0:1, :]
    h = h.astype(BF16)
    kvw = N_KV_HEADS * hd
    k_off = 3 * cw + sw
    zc_ref[0] = _dot(h, wa_ref[0, :, 0:3 * cw]).astype(BF16)
    zs_ref[0] = _dot(h, wa_ref[0, :, 3 * cw:k_off]).astype(BF16)

    tm = x.shape[0]
    nt = (((1,), (1,)), ((), ()))
    vt_ref[0] = lax.dot_general(wbt_ref[0, aw:aw + kvw, :], h, nt, preferred_element_type=F32).astype(BF16)

    lane_lo = (lax.broadcasted_iota(jnp.int32, (tm, hd), 1) & 32) == 0
    k = _dot(h, wa_ref[0, :, k_off + aw:k_off + aw + kvw])
    for i in range(N_KV_HEADS):
        y = _rms(k[:, i * hd:(i + 1) * hd], kg_ref[0])
        if rope:
            y = y * cos_ref[...] + _swap32(y, lane_lo) * sin_ref[...]
        k_ref[0, :, i * hd:(i + 1) * hd] = y.astype(BF16)

    qscale = hd ** -0.5 * math.log2(math.e)
    qt = lax.dot_general(wbt_ref[0, 0:aw, :], h, nt, preferred_element_type=F32)
    for i in range(aw // hd):
        z = qt[i * hd:(i + 1) * hd, :]
        ms = jnp.mean(z * z, axis=0, keepdims=True)
        y = z * lax.rsqrt(ms + RMS_EPS) * qg_ref[0]
        if rope:
            y = y * cost_ref[...] + _swap32_rows(y) * sint_ref[...]
        qt_ref[0, i * hd:(i + 1) * hd, :] = (y * qscale).astype(BF16)


def _inproj(x, mod, gain, w_a, w_bt, q_gain_col, k_gain, tabs, layer, dims, rope, tm):
    b, t, d = x.shape
    cw, sw, aw, hd = dims
    kvw = N_KV_HEADS * hd
    cos, sin, cos_t, sin_t = tabs
    tok = lambda w: pl.BlockSpec((1, tm, w), lambda bi, i: (bi, i, 0))
    tok_t = lambda w: pl.BlockSpec((1, w, tm), lambda bi, i: (bi, 0, i))
    tab = pl.BlockSpec((tm, hd), lambda bi, i: (i, 0))
    tab_t = pl.BlockSpec((hd, tm), lambda bi, i: (0, i))
    sds = jax.ShapeDtypeStruct
    return pl.pallas_call(
        functools.partial(_inproj_kernel, cw=cw, sw=sw, aw=aw, hd=hd, rope=rope),
        grid=(b, t // tm),
        in_specs=[
            tok(d),
            pl.BlockSpec((1, MOD_ROWS, d), lambda bi, i: (bi, 0, 0)),
            _layer_spec((1, d), layer),
            _layer_spec((d, w_a.shape[-1]), layer),
            _layer_spec((aw + kvw, d), layer),
            _layer_spec((hd, 1), layer),
            _layer_spec((1, hd), layer),
            tab, tab, tab_t, tab_t,
        ],
        out_specs=[tok(3 * cw), tok(sw), tok_t(aw), tok(kvw), tok_t(kvw)],
        out_shape=[sds((b, t, 3 * cw), BF16), sds((b, t, sw), BF16), sds((b, aw, t), BF16),
                   sds((b, t, kvw), BF16), sds((b, kvw, t), BF16)],
        compiler_params=_params(("arbitrary", "arbitrary")),
        name="inproj",
    )(x, mod, gain, w_a, w_bt, q_gain_col, k_gain, cos, sin, cos_t, sin_t)


def _s5_kernel(x_ref, atc_ref, bfc_ref, bbc_ref, cfc_ref, cbc_ref, ra_ref, rb_ref, lam_ref, h0_ref,
               y_ref, hfin_ref,
               at_ref, bf_ref, bb_ref, cf_ref, cb_ref, s_ref, hb_all_ref, h_ref,
               *, cb, nb, nblk, n_shift, p_shift):
    s = pl.program_id(1)
    half = h_ref.shape[-1] // 2
    blk = cb * nb

    def expand(comp_ref, r_ref, row_shift, col_shift):
        full = _dot(comp_ref[0, 0], r_ref[...])
        rg = (lax.broadcasted_iota(jnp.int32, full.shape, 0) >> row_shift) & (OCTET - 1)
        cg = (lax.broadcasted_iota(jnp.int32, full.shape, 1) >> col_shift) & (OCTET - 1)
        return jnp.where(rg == cg, full, 0.0).astype(at_ref.dtype)

    @pl.when(s == 0)
    def _():
        at_ref[...] = expand(atc_ref, ra_ref, n_shift, n_shift)
        bf_ref[...] = expand(bfc_ref, rb_ref, n_shift, p_shift)
        bb_ref[...] = expand(bbc_ref, rb_ref, n_shift, p_shift)
        cf_ref[...] = expand(cfc_ref, ra_ref, p_shift, n_shift)
        cb_ref[...] = expand(cbc_ref, ra_ref, p_shift, n_shift)
        h_ref[...] = h0_ref[1, 0]

    def scan(lam, reverse):
        lr = jnp.broadcast_to(lam[:, :half], (nb, half))
        li = jnp.broadcast_to(lam[:, half:], (nb, half))

        def body(k, carry):
            hr, hi = carry
            c = (cb - 1 - k) if reverse else k
            r0 = pl.multiple_of(c * nb, nb)
            srow = s_ref[pl.ds(r0, nb), :]
            s_ref[pl.ds(r0, nb), :] = jnp.concatenate([hr, hi], axis=1)
            nhr = lr * hr - li * hi + srow[:, :half]
            nhi = lr * hi + li * hr + srow[:, half:]
            return nhr, nhi

        hr, hi = lax.fori_loop(0, cb, body, (h_ref[:, :half], h_ref[:, half:]), unroll=2)
        h_ref[...] = jnp.concatenate([hr, hi], axis=1)

    @pl.when(s < nblk)
    def _():
        s_ref[...] = _dot(x_ref[0], bb_ref[...])
        scan(lam_ref[1, 0, 0], True)
        r0 = pl.multiple_of((nblk - 1 - s) * blk, blk)
        hb_all_ref[pl.ds(r0, blk), :] = s_ref[...].astype(hb_all_ref.dtype)

    @pl.when(s == nblk - 1)
    def _():
        hfin_ref[1, 0] = h_ref[...]
        h_ref[...] = h0_ref[0, 0]

    @pl.when(s >= nblk)
    def _():
        x = x_ref[0]
        s_ref[...] = _dot(x, bf_ref[...])
        scan(lam_ref[0, 0, 0], False)
        r0 = pl.multiple_of((s - nblk) * blk, blk)
        y = (_dot(x, at_ref[...]) + _dot(s_ref[...].astype(cf_ref.dtype), cf_ref[...])
             + _dot(hb_all_ref[pl.ds(r0, blk), :], cb_ref[...]))
        y_ref[0] = y.astype(y_ref.dtype)

    @pl.when(s == 2 * nblk - 1)
    def _():
        hfin_ref[0, 0] = h_ref[...]


def _s5(xq, ops, h0, layer, nb, cb):
    at_c, bf_c, bb_c, cf_c, cb_c, r_a, r_b, lam8, n_shift, p_shift = ops
    n_oct, rows, kx = xq.shape
    sw = lam8.shape[-1]
    blk = cb * nb
    nblk = rows // blk
    xblk = lambda o, s: (o, jnp.where(s < nblk, nblk - 1 - s, s - nblk), 0)
    yblk = lambda o, s: (o, jnp.maximum(s - nblk, 0), 0)
    comp = lambda a: pl.BlockSpec((1, 1) + a.shape[2:], lambda o, s: (layer, o, 0, 0))
    whole = lambda a: pl.BlockSpec(a.shape, lambda o, s: (0, 0))
    dt = xq.dtype
    return pl.pallas_call(
        functools.partial(_s5_kernel, cb=cb, nb=nb, nblk=nblk, n_shift=n_shift, p_shift=p_shift),
        grid=(n_oct, 2 * nblk),
        in_specs=[
            pl.BlockSpec((1, blk, kx), xblk),
            comp(at_c), comp(bf_c), comp(bb_c), comp(cf_c), comp(cb_c), whole(r_a), whole(r_b),
            pl.BlockSpec((2, 1, 1, 1, sw), lambda o, s: (0, layer, o, 0, 0)),
            pl.BlockSpec((2, 1, nb, sw), lambda o, s: (0, o, 0, 0)),
        ],
        out_specs=[
            pl.BlockSpec((1, blk, kx), yblk),
            pl.BlockSpec((2, 1, nb, sw), lambda o, s: (0, o, 0, 0)),
        ],
        out_shape=[
            jax.ShapeDtypeStruct((n_oct, rows, kx), dt),
            jax.ShapeDtypeStruct((2, n_oct, nb, sw), F32),
        ],
        scratch_shapes=[
            pltpu.VMEM((kx, kx), dt), pltpu.VMEM((kx, sw), dt), pltpu.VMEM((kx, sw), dt),
            pltpu.VMEM((sw, kx), dt), pltpu.VMEM((sw, kx), dt),
            pltpu.VMEM((blk, sw), F32), pltpu.VMEM((rows, sw), dt), pltpu.VMEM((nb, sw), F32),
        ],
        compiler_params=_params(("arbitrary", "arbitrary")),
        name="s5_mixer",
    )(xq, at_c, bf_c, bb_c, cf_c, cb_c, r_a, r_b, lam8, h0)


def _cmul(a, b):
    return a[0] * b[0] - a[1] * b[1], a[0] * b[1] + a[1] * b[0]


def _s5_operators(lam_re, lam_im, log_dt, b_re, b_im, c_re, c_im):
    hp = lax.Precision.HIGHEST
    depth, _, g, p = lam_re.shape
    n = b_re.shape[-1]
    n_oct = g // OCTET
    m = S5_CHUNK
    assert n & (n - 1) == 0 and p & (p - 1) == 0 and m * n == LANES and 2 * p == LANES
    dt = jnp.exp(log_dt)[..., None]
    mag = jnp.exp(lam_re * dt)
    lam_bar = (mag * jnp.cos(lam_im * dt), mag * jnp.sin(lam_im * dt))
    den = lam_re * lam_re + lam_im * lam_im
    num = (lam_bar[0] - 1.0, lam_bar[1])
    ratio = ((num[0] * lam_re + num[1] * lam_im) / den, (num[1] * lam_re - num[0] * lam_im) / den)
    bbar = _cmul((ratio[0][..., None], ratio[1][..., None]), (b_re, b_im))
    pows = [(jnp.ones_like(lam_re), jnp.zeros_like(lam_re))]
    for _ in range(m):
        pows.append(_cmul(pows[-1], lam_bar))
    pw = (jnp.stack([q[0] for q in pows]), jnp.stack([q[1] for q in pows]))

    def compact_in(e, d, exps):
        sel = jnp.stack([e[k][:, d] for k in exps]).reshape(m, depth, n_oct, OCTET, p, n)
        return jnp.transpose(sel, (1, 2, 0, 3, 5, 4)).reshape(depth, n_oct, m * OCTET * n, p)

    def compact_out(f, d, exps):
        sel = jnp.stack([f[k][:, d] for k in exps]).reshape(m, depth, n_oct, OCTET, n, p)
        return jnp.transpose(sel, (1, 2, 3, 5, 0, 4)).reshape(depth, n_oct, OCTET * p, m * n)

    e = [_cmul((pw[0][k][..., None], pw[1][k][..., None]), bbar) for k in range(m + 1)]
    f = [_cmul((c_re, c_im), (pw[0][k][:, :, :, None, :], pw[1][k][:, :, :, None, :])) for k in range(m + 1)]
    er, ei = [q[0] for q in e], [q[1] for q in e]
    fr, fi = [q[0] for q in f], [-q[1] for q in f]
    fexp = [m - 1 - i for i in range(m)]
    bexp = list(range(m))
    fout = [j + 1 for j in range(m)]
    bout = [m - j for j in range(m)]
    b_f = jnp.concatenate([compact_in(er, 0, fexp), compact_in(ei, 0, fexp)], axis=-1)
    b_b = jnp.concatenate([compact_in(er, 1, bexp), compact_in(ei, 1, bexp)], axis=-1)
    c_f = jnp.concatenate([compact_out(fr, 0, fout), compact_out(fi, 0, fout)], axis=-2)
    c_b = jnp.concatenate([compact_out(fr, 1, bout), compact_out(fi, 1, bout)], axis=-2)

    kk = jnp.stack([
        jnp.einsum('ldgop,ldgpn->ldgon', f[k][0], bbar[0], precision=hp)
        - jnp.einsum('ldgop,ldgpn->ldgon', f[k][1], bbar[1], precision=hp)
        for k in range(m)])
    tau = jnp.arange(m)[None, :] - jnp.arange(m)[:, None]
    kf = kk[:, :, 0][jnp.clip(tau, 0, m - 1)]
    kb = kk[:, :, 1][jnp.clip(-tau, 0, m - 1)]
    tmask = lambda c: c[:, :, None, None, None, None]
    toe = jnp.where(tmask(tau >= 0), kf, 0.0) + jnp.where(tmask(tau <= 0), kb, 0.0)
    toe = toe.reshape(m, m, depth, n_oct, OCTET, n, n)
    a_tot = jnp.transpose(toe, (2, 3, 0, 4, 6, 1, 5)).reshape(depth, n_oct, m * OCTET * n, m * n)

    col = jnp.arange(OCTET * LANES)
    row = jnp.arange(LANES)
    r_a = (row[:, None] // n == col[None, :] // (OCTET * n)) & (row[:, None] % n == col[None, :] % n)
    r_b = (row[:, None] // p == col[None, :] // (OCTET * p)) & (row[:, None] % p == col[None, :] % p)

    lam8 = jnp.concatenate([pw[0][m].reshape(depth, 2, n_oct, 1, OCTET * p),
                            pw[1][m].reshape(depth, 2, n_oct, 1, OCTET * p)], axis=-1)
    lam8 = jnp.moveaxis(lam8, 1, 0)
    cast = lambda a: a.astype(BF16)
    return (cast(a_tot), cast(b_f), cast(b_b), cast(c_f), cast(c_b), cast(r_a), cast(r_b), lam8,
            n.bit_length() - 1, p.bit_length() - 1)


def _to_chunk_layout(u):
    b, t, w = u.shape
    n_oct = w // LANES
    u = u.reshape(b, t // S5_CHUNK, S5_CHUNK, n_oct, LANES)
    u = jnp.transpose(u, (3, 1, 0, 2, 4))
    return u.reshape(n_oct, (t // S5_CHUNK) * b, S5_CHUNK * LANES)


def _from_chunk_layout(y, b):
    n_oct, rows, _ = y.shape
    mc = rows // b
    y = y.reshape(n_oct, mc, b, S5_CHUNK, LANES)
    y = jnp.transpose(y, (2, 1, 3, 0, 4))
    return y.reshape(b, mc * S5_CHUNK, n_oct * LANES)


def _attn_kernel(*refs, group, hd, chunks):
    qt_ref = refs[0]
    ot_ref = refs[-1]
    kv = refs[1:-1]
    tq = qt_ref.shape[2]
    qt = jnp.concatenate([qt_ref[0, h * hd:(h + 1) * hd, :] for h in range(group)], axis=1)
    cols = group * tq
    m = jnp.full((1, cols), -jnp.inf, F32)
    acc = jnp.zeros((hd + BF16_ROWS, cols), F32)
    for i, tk in enumerate(chunks):
        k_ref, vt_ref = kv[2 * i], kv[2 * i + 1]
        ones = jnp.ones((BF16_ROWS, tk), BF16)
        for c in range(k_ref.shape[1] // tk):
            k = k_ref[0, c * tk:(c + 1) * tk, :]
            vt = jnp.concatenate([vt_ref[0, :, c * tk:(c + 1) * tk], ones], axis=0)
            st = _dot(k, qt)
            m_new = jnp.maximum(m, jnp.max(st, axis=0, keepdims=True))
            alpha = jnp.exp2(m - m_new)
            p = jnp.exp2(st - m_new)
            acc = alpha * acc + _dot(vt, p.astype(BF16))
            m = m_new
    ot = acc[:hd] / acc[hd:hd + 1]
    for h in range(group):
        ot_ref[0, h * hd:(h + 1) * hd, :] = ot[:, h * tq:(h + 1) * tq].astype(BF16)


def _attention(qt, kvs, hd, tq):
    b, aw, t = qt.shape
    group = aw // hd // N_KV_HEADS
    gw = group * hd
    chunks = tuple(min(k.shape[1], 512) for k, _ in kvs)
    in_specs = [pl.BlockSpec((1, gw, tq), lambda bi, j, i: (bi, j, i))]
    args = [qt]
    for k, vt in kvs:
        s = k.shape[1]
        in_specs.append(pl.BlockSpec((1, s, hd), lambda bi, j, i: (bi, 0, j)))
        in_specs.append(pl.BlockSpec((1, hd, s), lambda bi, j, i: (bi, j, 0)))
        args += [k, vt]
    return pl.pallas_call(
        functools.partial(_attn_kernel, group=group, hd=hd, chunks=chunks),
        grid=(b, N_KV_HEADS, t // tq),
        in_specs=in_specs,
        out_specs=pl.BlockSpec((1, gw, tq), lambda bi, j, i: (bi, j, i)),
        out_shape=jax.ShapeDtypeStruct((b, aw, t), BF16),
        compiler_params=_params(("arbitrary", "arbitrary", "arbitrary")),
        name="attention",
    )(*args)


def _outproj_kernel(zc_ref, zp_ref, zn_ref, y_ref, u_ref, at_ref, x_ref, mod_ref,
                    cwt_ref, d_ref, wglu_ref, bglu_ref, wo_ref, g_ref, o_ref, ubuf_ref, *, cw):
    i = pl.program_id(1)
    tm = x_ref.shape[1]

    def gated(z):
        z = z.astype(F32)
        return z[:, 2 * cw:] * z[:, :cw]

    zc = zc_ref[0].astype(F32)
    u = zc[:, 2 * cw:] * zc[:, :cw]
    prev = gated(zp_ref[0])[BF16_ROWS - 1:BF16_ROWS]
    nxt = gated(zn_ref[0])[0:1]
    ubuf_ref[SUBLANES - 1:SUBLANES, :] = jnp.where(i > 0, prev, 0.0)
    ubuf_ref[SUBLANES:SUBLANES + tm, :] = u
    ubuf_ref[SUBLANES + tm:SUBLANES + tm + 1, :] = jnp.where(i < pl.num_programs(1) - 1, nxt, 0.0)
    y = (cwt_ref[0, 0:1] * ubuf_ref[SUBLANES - 1:SUBLANES - 1 + tm, :] + cwt_ref[0, 1:2] * u
         + cwt_ref[0, 2:3] * ubuf_ref[SUBLANES + 1:SUBLANES + 1 + tm, :])
    conv = (zc[:, cw:2 * cw] * y).astype(BF16)

    ys = y_ref[0].astype(F32) + d_ref[0] * u_ref[0].astype(F32)
    gl = jax.nn.gelu(ys)
    gate = jax.nn.sigmoid(_dot(gl.astype(BF16), wglu_ref[0]) + bglu_ref[0])
    ssm = (gl * gate).astype(BF16)

    ns = conv.shape[1] + ssm.shape[1]
    tn = (((0,), (0,)), ((), ()))
    mix = (_dot(jnp.concatenate([conv, ssm], axis=1), wo_ref[0, 0:ns, :])
           + lax.dot_general(at_ref[0], wo_ref[0, ns:, :], tn, preferred_element_type=F32))
    o_ref[0] = x_ref[0] + mod_ref[0, 2:3, :] * _rms(mix, g_ref[0])


def _outproj(zc, y, u, at, x, mod, conv_w, ssm_d, w_glu, b_glu, w_out, gain, layer, dims, tm):
    b, t, d = x.shape
    cw, sw, aw, hd = dims
    nhalo = tm // BF16_ROWS
    last = t // BF16_ROWS - 1
    tok = lambda w: pl.BlockSpec((1, tm, w), lambda bi, i: (bi, i, 0))
    return pl.pallas_call(
        functools.partial(_outproj_kernel, cw=cw),
        grid=(b, t // tm),
        in_specs=[
            tok(3 * cw),
            pl.BlockSpec((1, BF16_ROWS, 3 * cw), lambda bi, i: (bi, jnp.maximum(i * nhalo - 1, 0), 0)),
            pl.BlockSpec((1, BF16_ROWS, 3 * cw), lambda bi, i: (bi, jnp.minimum((i + 1) * nhalo, last), 0)),
            tok(sw), tok(sw),
            pl.BlockSpec((1, aw, tm), lambda bi, i: (bi, 0, i)),
            tok(d),
            pl.BlockSpec((1, MOD_ROWS, d), lambda bi, i: (bi, 0, 0)),
            _layer_spec((conv_w.shape[1], cw), layer),
            _layer_spec((1, sw), layer),
            _layer_spec((sw, sw), layer),
            _layer_spec((1, sw), layer),
            _layer_spec((d, d), layer),
            _layer_spec((1, d), layer),
        ],
        out_specs=tok(d),
        out_shape=jax.ShapeDtypeStruct((b, t, d), F32),
        scratch_shapes=[pltpu.VMEM((tm + 2 * SUBLANES, cw), F32)],
        compiler_params=_params(("arbitrary", "arbitrary")),
        name="outproj",
    )(zc, zc, zc, y, u, at, x, mod, conv_w, ssm_d, w_glu, b_glu, w_out, gain)


def _ffn_kernel(x_ref, mod_ref, gpre_ref, gpost_ref, wg_ref, wu_ref, wd_ref, o_ref, h_ref, acc_ref, *, nj):
    j = pl.program_id(2)

    def hidden_tile(h):
        a = jax.nn.silu(_dot(h, wg_ref[0])) * _dot(h, wu_ref[0])
        return _dot(a.astype(BF16), wd_ref[0])

    def pre_norm():
        gain = gpre_ref[0] * (1.0 + mod_ref[0, 4:5, :])
        h = (_rms(x_ref[0], gain) + mod_ref[0, 3:4, :]).astype(BF16)
        h_ref[...] = h
        return h

    def post_norm(acc):
        o_ref[0] = x_ref[0] + mod_ref[0, 5:6, :] * _rms(acc, gpost_ref[0])

    if nj == 1:
        post_norm(hidden_tile(pre_norm()))
        return

    @pl.when(j == 0)
    def _():
        acc_ref[...] = hidden_tile(pre_norm())

    @pl.when((j > 0) & (j < nj - 1))
    def _():
        acc_ref[...] += hidden_tile(h_ref[...])

    @pl.when(j == nj - 1)
    def _():
        post_norm(acc_ref[...] + hidden_tile(h_ref[...]))


def _ffn(x, mod, g_pre, g_post, w_gate, w_up, w_down, layer, tm, th):
    b, t, d = x.shape
    f = w_gate.shape[-1]
    tok = pl.BlockSpec((1, tm, d), lambda bi, i, j: (bi, i, 0))
    return pl.pallas_call(
        functools.partial(_ffn_kernel, nj=f // th),
        grid=(b, t // tm, f // th),
        in_specs=[
            tok,
            pl.BlockSpec((1, MOD_ROWS, d), lambda bi, i, j: (bi, 0, 0)),
            _layer_spec((1, d), layer),
            _layer_spec((1, d), layer),
            pl.BlockSpec((1, d, th), lambda bi, i, j: (layer, 0, j)),
            pl.BlockSpec((1, d, th), lambda bi, i, j: (layer, 0, j)),
            pl.BlockSpec((1, th, d), lambda bi, i, j: (layer, j, 0)),
        ],
        out_specs=tok,
        out_shape=jax.ShapeDtypeStruct((b, t, d), F32),
        scratch_shapes=[pltpu.VMEM((tm, d), BF16), pltpu.VMEM((tm, d), F32)],
        input_output_aliases={0: 0},
        compiler_params=_params(("arbitrary", "arbitrary", "arbitrary")),
        name="ffn",
    )(x, mod, g_pre, g_post, w_gate, w_up, w_down)


def _rope_tables(t, hd):
    rows = t // GRID_W
    row = jnp.broadcast_to(jnp.arange(rows)[:, None], (rows, GRID_W)).reshape(-1).astype(F32)
    col = jnp.broadcast_to(jnp.arange(GRID_W)[None, :], (rows, GRID_W)).reshape(-1).astype(F32)
    half = hd // 2
    inv_freq = ROPE_THETA ** (-jnp.arange(0, half, 2, dtype=F32) / half)
    ar = row[:, None] * inv_freq
    ac = col[:, None] * inv_freq
    cos = jnp.concatenate([jnp.cos(ar), jnp.cos(ar), jnp.cos(ac), jnp.cos(ac)], axis=-1)
    sin = jnp.concatenate([-jnp.sin(ar), jnp.sin(ar), -jnp.sin(ac), jnp.sin(ac)], axis=-1)
    return cos, sin


def _tile(n, pref):
    return pref if n % pref == 0 else n


def kernel(x, c, ctx, c_ctx, w_mod, b_mod, g_pre_mix, g_post_mix, g_pre_ffn, g_post_ffn, w_in, conv_w, ssm_lam_re, ssm_lam_im, ssm_log_dt, ssm_b_re, ssm_b_im, ssm_c_re, ssm_c_im, ssm_d, w_glu, b_glu, q_norm, k_norm, w_out, w_gate, w_up, w_down):
    bsz, t, d = x.shape
    n_ctx = ctx.shape[1]
    depth = w_mod.shape[0]
    cw = conv_w.shape[-1]
    sw = w_glu.shape[-1]
    hd = q_norm.shape[-1]
    aw = d - cw - sw
    dims = (cw, sw, aw, hd)
    assert w_in.shape[-1] == 3 * cw + sw + aw + 2 * N_KV_HEADS * hd
    assert sw % LANES == 0 and t % (S5_CHUNK * BF16_ROWS) == 0 and n_ctx % (S5_CHUNK * BF16_ROWS) == 0

    cvec = jnp.concatenate([c, c_ctx[None, :], jnp.zeros((BF16_ROWS - bsz - 1, d), F32)], axis=0)
    mod = _modulation(cvec, w_mod, b_mod).reshape(depth, BF16_ROWS, N_MOD, d)
    mod = jnp.pad(mod, ((0, 0), (0, 0), (0, MOD_ROWS - N_MOD), (0, 0)))
    mod_x = mod[:, :bsz]
    mod_c = jnp.broadcast_to(mod[:, bsz:bsz + 1], (depth, bsz, MOD_ROWS, d))

    cos, sin = _rope_tables(t, hd)
    tabs = (cos, sin, cos.T, sin.T)
    ops = _s5_operators(ssm_lam_re, ssm_lam_im, ssm_log_dt, ssm_b_re, ssm_b_im, ssm_c_re, ssm_c_im)
    n_oct = sw // LANES

    r3 = lambda a: a.reshape(depth, 1, a.shape[-1])
    g_pre_mix, g_post_mix, g_pre_ffn, g_post_ffn = map(r3, (g_pre_mix, g_post_mix, g_pre_ffn, g_post_ffn))
    q_col = q_norm.reshape(depth, hd, 1)
    k_norm, ssm_d, b_glu = map(r3, (k_norm, ssm_d, b_glu))
    q_off = 3 * cw + sw
    k_off = q_off + aw
    v_off = k_off + N_KV_HEADS * hd
    w_a = w_in.astype(BF16)
    w_bt = jnp.swapaxes(jnp.concatenate([w_a[..., q_off:k_off], w_a[..., v_off:]], axis=-1), 1, 2)
    w_glu, w_out, w_gate, w_up, w_down = (w.astype(BF16) for w in (w_glu, w_out, w_gate, w_up, w_down))

    tm_x, tm_c = _tile(t, 512), _tile(n_ctx, 256)
    tq_x, tq_c = _tile(t, 1024), _tile(n_ctx, 256)
    f = w_gate.shape[-1]
    th = max(c for c in range(LANES, 512 + 1, LANES) if f % c == 0)
    cb_x = _tile(t // S5_CHUNK, 128)
    cb_c = n_ctx // S5_CHUNK
    h_zero = jnp.zeros((2, n_oct, bsz, 2 * OCTET * ssm_lam_re.shape[-1]), F32)

    def s5(zs, h0, layer, cb):
        y, hfin = _s5(_to_chunk_layout(zs), ops, h0, layer, bsz, cb)
        return _from_chunk_layout(y, bsz), hfin

    xc = ctx
    for l in range(depth):
        want_ctx = l < depth - 1
        zc_x, zs_x, q_x, k_x, v_x = _inproj(x, mod_x[l], g_pre_mix, w_a, w_bt, q_col, k_norm, tabs, l, dims, True, tm_x)
        zc_c, zs_c, q_c, k_c, v_c = _inproj(xc, mod_c[l], g_pre_mix, w_a, w_bt, q_col, k_norm, tabs, l, dims, False, tm_c)
        y_c, h_ctx = s5(zs_c, h_zero, l, cb_c)
        y_x, _ = s5(zs_x, h_ctx, l, cb_x)
        at_x = _attention(q_x, [(k_x, v_x), (k_c, v_c)], hd, tq_x)
        x = _outproj(zc_x, y_x, zs_x, at_x, x, mod_x[l], conv_w, ssm_d, w_glu, b_glu, w_out,
                     g_post_mix, l, dims, tm_x)
        x = _ffn(x, mod_x[l], g_pre_ffn, g_post_ffn, w_gate, w_up, w_down, l, tm_x, th)
        if want_ctx:
            at_c = _attention(q_c, [(k_c, v_c)], hd, tq_c)
            xc = _outproj(zc_c, y_c, zs_c, at_c, xc, mod_c[l], conv_w, ssm_d, w_glu, b_glu, w_out,
                          g_post_mix, l, dims, tm_c)
            xc = _ffn(xc, mod_c[l], g_pre_ffn, g_post_ffn, w_gate, w_up, w_down, l, tm_c, th)
    return x
```

```python
import functools
import math

import jax
import jax.numpy as jnp
from jax import lax
from jax.experimental import pallas as pl
from jax.experimental.pallas import tpu as pltpu

F32 = jnp.float32
BF16 = jnp.bfloat16

GRID_W = 64
N_KV_HEADS = 2
N_MOD = 6
ROPE_THETA = 10000.0
RMS_EPS = 1e-6

LANES = 128
SUBLANES = 8
BF16_ROWS = 16
V7X_VMEM_LIMIT_BYTES = 56 * 1024 * 1024

S5_CHUNK = 8
OCTET = LANES // 16

MOD_ROWS = 8

def _dot(a, b):
    return jnp.dot(a, b, preferred_element_type=F32)


def _rms(x, gain):
    ms = jnp.mean(x * x, axis=-1, keepdims=True)
    return x * lax.rsqrt(ms + RMS_EPS) * gain


def _const_spec(shape):
    nd = len(shape)
    return pl.BlockSpec(shape, lambda *_: (0,) * nd, pipeline_mode=pl.Buffered(1))


def _layer_spec(shape, layer):
    nd = len(shape)
    return pl.BlockSpec((1,) + shape, lambda *_: (layer,) + (0,) * nd, pipeline_mode=pl.Buffered(1))


def _params(semantics):
    return pltpu.CompilerParams(dimension_semantics=semantics, vmem_limit_bytes=V7X_VMEM_LIMIT_BYTES)


def _mod_kernel(c_ref, w_ref, b_ref, o_ref):
    s = jax.nn.silu(c_ref[...]).astype(BF16)
    o_ref[0] = _dot(s, w_ref[0].astype(BF16)) + b_ref[0]


def _modulation(cvec, w_mod, b_mod):
    depth, d, n = w_mod.shape
    r = cvec.shape[0]
    tn = min(n, 1024)
    return pl.pallas_call(
        _mod_kernel,
        grid=(depth, n // tn),
        in_specs=[
            pl.BlockSpec((r, d), lambda l, j: (0, 0)),
            pl.BlockSpec((1, d, tn), lambda l, j: (l, 0, j)),
            pl.BlockSpec((1, 1, tn), lambda l, j: (l, 0, j)),
        ],
        out_specs=pl.BlockSpec((1, r, tn), lambda l, j: (l, 0, j)),
        out_shape=jax.ShapeDtypeStruct((depth, r, n), F32),
        compiler_params=_params(("arbitrary", "arbitrary")),
        name="modulation",
    )(cvec, w_mod, b_mod.reshape(depth, 1, n))


def _wt_kernel(blk_ref, w_ref, o_ref):
    del blk_ref
    o_ref[0] = w_ref[0].T.astype(BF16)


def _transposed_columns(w, col_blocks, tn):
    depth, d, _ = w.shape
    blocks = jnp.asarray(col_blocks, jnp.int32)
    return pl.pallas_call(
        _wt_kernel,
        grid_spec=pltpu.PrefetchScalarGridSpec(
            num_scalar_prefetch=1,
            grid=(depth, len(col_blocks)),
            in_specs=[pl.BlockSpec((1, d, tn), lambda l, j, blk: (l, 0, blk[j]))],
            out_specs=pl.BlockSpec((1, tn, d), lambda l, j, blk: (l, j, 0)),
        ),
        out_shape=jax.ShapeDtypeStruct((depth, len(col_blocks) * tn, d), BF16),
        compiler_params=_params(("arbitrary", "arbitrary")),
        name="transpose_weights",
    )(blocks, w)


def _swap32(y, lane_lo):
    return jnp.where(lane_lo, pltpu.roll(y, LANES - 32, 1), pltpu.roll(y, 32, 1))


def _swap32_rows(y):
    return jnp.concatenate([y[32:64], y[0:32], y[96:128], y[64:96]], axis=0)


def _inproj_kernel(x_ref, mod_ref, g_ref, wa_ref, wbt_ref, qg_ref, kg_ref, cos_ref, sin_ref, cost_ref, sint_ref,
                   zc_ref, zs_ref, qt_ref, k_ref, vt_ref, *, cw, sw, aw, hd, rope):
    x = x_ref[0]
    h = _rms(x, g_ref[0] * (1.0 + mod_ref[0, 1:2, :])) + mod_ref[0, 0:1, :]
    h = h.astype(BF16)
    kvw = N_KV_HEADS * hd
    k_off = 3 * cw + sw
    zc_ref[0] = _dot(h, wa_ref[0, :, 0:3 * cw]).astype(BF16)
    zs_ref[0] = _dot(h, wa_ref[0, :, 3 * cw:k_off]).astype(BF16)

    tm = x.shape[0]
    nt = (((1,), (1,)), ((), ()))
    vt_ref[0] = lax.dot_general(wbt_ref[0, aw:aw + kvw, :], h, nt, preferred_element_type=F32).astype(BF16)

    lane_lo = (lax.broadcasted_iota(jnp.int32, (tm, hd), 1) & 32) == 0
    k = _dot(h, wa_ref[0, :, k_off + aw:k_off + aw + kvw])
    for i in range(N_KV_HEADS):
        y = _rms(k[:, i * hd:(i + 1) * hd], kg_ref[0])
        if rope:
            y = y * cos_ref[...] + _swap32(y, lane_lo) * sin_ref[...]
        k_ref[0, :, i * hd:(i + 1) * hd] = y.astype(BF16)

    qscale = hd ** -0.5 * math.log2(math.e)
    qt = lax.dot_general(wbt_ref[0, 0:aw, :], h, nt, preferred_element_type=F32)
    for i in range(aw // hd):
        z = qt[i * hd:(i + 1) * hd, :]
        ms = jnp.mean(z * z, axis=0, keepdims=True)
        y = z * lax.rsqrt(ms + RMS_EPS) * qg_ref[0]
        if rope:
            y = y * cost_ref[...] + _swap32_rows(y) * sint_ref[...]
        qt_ref[0, i * hd:(i + 1) * hd, :] = (y * qscale).astype(BF16)


def _inproj(x, mod, gain, w_a, w_bt, q_gain_col, k_gain, tabs, layer, dims, rope, tm):
    b, t, d = x.shape
    cw, sw, aw, hd = dims
    kvw = N_KV_HEADS * hd
    cos, sin, cos_t, sin_t = tabs
    tok = lambda w: pl.BlockSpec((1, tm, w), lambda bi, i: (bi, i, 0))
    tok_t = lambda w: pl.BlockSpec((1, w, tm), lambda bi, i: (bi, 0, i))
    tab = pl.BlockSpec((tm, hd), lambda bi, i: (i, 0))
    tab_t = pl.BlockSpec((hd, tm), lambda bi, i: (0, i))
    sds = jax.ShapeDtypeStruct
    return pl.pallas_call(
        functools.partial(_inproj_kernel, cw=cw, sw=sw, aw=aw, hd=hd, rope=rope),
        grid=(b, t // tm),
        in_specs=[
            tok(d),
            pl.BlockSpec((1, MOD_ROWS, d), lambda bi, i: (bi, 0, 0)),
            _layer_spec((1, d), layer),
            _layer_spec((d, w_a.shape[-1]), layer),
            _layer_spec((aw + kvw, d), layer),
            _layer_spec((hd, 1), layer),
            _layer_spec((1, hd), layer),
            tab, tab, tab_t, tab_t,
        ],
        out_specs=[tok(3 * cw), tok(sw), tok_t(aw), tok(kvw), tok_t(kvw)],
        out_shape=[sds((b, t, 3 * cw), BF16), sds((b, t, sw), BF16), sds((b, aw, t), BF16),
                   sds((b, t, kvw), BF16), sds((b, kvw, t), BF16)],
        compiler_params=_params(("arbitrary", "arbitrary")),
        name="inproj",
    )(x, mod, gain, w_a, w_bt, q_gain_col, k_gain, cos, sin, cos_t, sin_t)


def _s5_kernel(x_ref, atc_ref, bfc_ref, bbc_ref, cfc_ref, cbc_ref, ra_ref, rb_ref, lam_ref, h0_ref,
               y_ref, hfin_ref,
               at_ref, bf_ref, bb_ref, cf_ref, cb_ref, s_ref, hb_all_ref, h_ref,
               *, cb, nb, nblk, n_shift, p_shift):
    s = pl.program_id(1)
    half = h_ref.shape[-1] // 2
    blk = cb * nb

    def expand(comp_ref, r_ref, row_shift, col_shift):
        full = _dot(comp_ref[0, 0], r_ref[...])
        rg = (lax.broadcasted_iota(jnp.int32, full.shape, 0) >> row_shift) & (OCTET - 1)
        cg = (lax.broadcasted_iota(jnp.int32, full.shape, 1) >> col_shift) & (OCTET - 1)
        return jnp.where(rg == cg, full, 0.0).astype(at_ref.dtype)

    @pl.when(s == 0)
    def _():
        at_ref[...] = expand(atc_ref, ra_ref, n_shift, n_shift)
        bf_ref[...] = expand(bfc_ref, rb_ref, n_shift, p_shift)
        bb_ref[...] = expand(bbc_ref, rb_ref, n_shift, p_shift)
        cf_ref[...] = expand(cfc_ref, ra_ref, p_shift, n_shift)
        cb_ref[...] = expand(cbc_ref, ra_ref, p_shift, n_shift)
        h_ref[...] = h0_ref[1, 0]

    def scan(lam, reverse):
        lr = jnp.broadcast_to(lam[:, :half], (nb, half))
        li = jnp.broadcast_to(lam[:, half:], (nb, half))

        def body(k, carry):
            hr, hi = carry
            c = (cb - 1 - k) if reverse else k
            r0 = pl.multiple_of(c * nb, nb)
            srow = s_ref[pl.ds(r0, nb), :]
            s_ref[pl.ds(r0, nb), :] = jnp.concatenate([hr, hi], axis=1)
            nhr = lr * hr - li * hi + srow[:, :half]
            nhi = lr * hi + li * hr + srow[:, half:]
            return nhr, nhi

        hr, hi = lax.fori_loop(0, cb, body, (h_ref[:, :half], h_ref[:, half:]), unroll=2)
        h_ref[...] = jnp.concatenate([hr, hi], axis=1)

    @pl.when(s < nblk)
    def _():
        s_ref[...] = _dot(x_ref[0], bb_ref[...])
        scan(lam_ref[1, 0, 0], True)
        r0 = pl.multiple_of((nblk - 1 - s) * blk, blk)
        hb_all_ref[pl.ds(r0, blk), :] = s_ref[...].astype(hb_all_ref.dtype)

    @pl.when(s == nblk - 1)
    def _():
        hfin_ref[1, 0] = h_ref[...]
        h_ref[...] = h0_ref[0, 0]

    @pl.when(s >= nblk)
    def _():
        x = x_ref[0]
        s_ref[...] = _dot(x, bf_ref[...])
        scan(lam_ref[0, 0, 0], False)
        r0 = pl.multiple_of((s - nblk) * blk, blk)
        y = (_dot(x, at_ref[...]) + _dot(s_ref[...].astype(cf_ref.dtype), cf_ref[...])
             + _dot(hb_all_ref[pl.ds(r0, blk), :], cb_ref[...]))
        y_ref[0] = y.astype(y_ref.dtype)

    @pl.when(s == 2 * nblk - 1)
    def _():
        hfin_ref[0, 0] = h_ref[...]


def _s5(xq, ops, h0, layer, nb, cb):
    at_c, bf_c, bb_c, cf_c, cb_c, r_a, r_b, lam8, n_shift, p_shift = ops
    n_oct, rows, kx = xq.shape
    sw = lam8.shape[-1]
    blk = cb * nb
    nblk = rows // blk
    xblk = lambda o, s: (o, jnp.where(s < nblk, nblk - 1 - s, s - nblk), 0)
    yblk = lambda o, s: (o, jnp.maximum(s - nblk, 0), 0)
    comp = lambda a: pl.BlockSpec((1, 1) + a.shape[2:], lambda o, s: (layer, o, 0, 0))
    whole = lambda a: pl.BlockSpec(a.shape, lambda o, s: (0, 0))
    dt = xq.dtype
    return pl.pallas_call(
        functools.partial(_s5_kernel, cb=cb, nb=nb, nblk=nblk, n_shift=n_shift, p_shift=p_shift),
        grid=(n_oct, 2 * nblk),
        in_specs=[
            pl.BlockSpec((1, blk, kx), xblk),
            comp(at_c), comp(bf_c), comp(bb_c), comp(cf_c), comp(cb_c), whole(r_a), whole(r_b),
            pl.BlockSpec((2, 1, 1, 1, sw), lambda o, s: (0, layer, o, 0, 0)),
            pl.BlockSpec((2, 1, nb, sw), lambda o, s: (0, o, 0, 0)),
        ],
        out_specs=[
            pl.BlockSpec((1, blk, kx), yblk),
            pl.BlockSpec((2, 1, nb, sw), lambda o, s: (0, o, 0, 0)),
        ],
        out_shape=[
            jax.ShapeDtypeStruct((n_oct, rows, kx), dt),
            jax.ShapeDtypeStruct((2, n_oct, nb, sw), F32),
        ],
        scratch_shapes=[
            pltpu.VMEM((kx, kx), dt), pltpu.VMEM((kx, sw), dt), pltpu.VMEM((kx, sw), dt),
            pltpu.VMEM((sw, kx), dt), pltpu.VMEM((sw, kx), dt),
            pltpu.VMEM((blk, sw), F32), pltpu.VMEM((rows, sw), dt), pltpu.VMEM((nb, sw), F32),
        ],
        compiler_params=_params(("arbitrary", "arbitrary")),
        name="s5_mixer",
    )(xq, at_c, bf_c, bb_c, cf_c, cb_c, r_a, r_b, lam8, h0)


def _cmul(a, b):
    return a[0] * b[0] - a[1] * b[1], a[0] * b[1] + a[1] * b[0]


def _s5_operators(lam_re, lam_im, log_dt, b_re, b_im, c_re, c_im):
    hp = lax.Precision.HIGHEST
    depth, _, g, p = lam_re.shape
    n = b_re.shape[-1]
    n_oct = g // OCTET
    m = S5_CHUNK
    assert n & (n - 1) == 0 and p & (p - 1) == 0 and m * n == LANES and 2 * p == LANES
    dt = jnp.exp(log_dt)[..., None]
    mag = jnp.exp(lam_re * dt)
    lam_bar = (mag * jnp.cos(lam_im * dt), mag * jnp.sin(lam_im * dt))
    den = lam_re * lam_re + lam_im * lam_im
    num = (lam_bar[0] - 1.0, lam_bar[1])
    ratio = ((num[0] * lam_re + num[1] * lam_im) / den, (num[1] * lam_re - num[0] * lam_im) / den)
    bbar = _cmul((ratio[0][..., None], ratio[1][..., None]), (b_re, b_im))
    pows = [(jnp.ones_like(lam_re), jnp.zeros_like(lam_re))]
    for _ in range(m):
        pows.append(_cmul(pows[-1], lam_bar))
    pw = (jnp.stack([q[0] for q in pows]), jnp.stack([q[1] for q in pows]))

    def compact_in(e, d, exps):
        sel = jnp.stack([e[k][:, d] for k in exps]).reshape(m, depth, n_oct, OCTET, p, n)
        return jnp.transpose(sel, (1, 2, 0, 3, 5, 4)).reshape(depth, n_oct, m * OCTET * n, p)

    def compact_out(f, d, exps):
        sel = jnp.stack([f[k][:, d] for k in exps]).reshape(m, depth, n_oct, OCTET, n, p)
        return jnp.transpose(sel, (1, 2, 3, 5, 0, 4)).reshape(depth, n_oct, OCTET * p, m * n)

    e = [_cmul((pw[0][k][..., None], pw[1][k][..., None]), bbar) for k in range(m + 1)]
    f = [_cmul((c_re, c_im), (pw[0][k][:, :, :, None, :], pw[1][k][:, :, :, None, :])) for k in range(m + 1)]
    er, ei = [q[0] for q in e], [q[1] for q in e]
    fr, fi = [q[0] for q in f], [-q[1] for q in f]
    fexp = [m - 1 - i for i in range(m)]
    bexp = list(range(m))
    fout = [j + 1 for j in range(m)]
    bout = [m - j for j in range(m)]
    b_f = jnp.concatenate([compact_in(er, 0, fexp), compact_in(ei, 0, fexp)], axis=-1)
    b_b = jnp.concatenate([compact_in(er, 1, bexp), compact_in(ei, 1, bexp)], axis=-1)
    c_f = jnp.concatenate([compact_out(fr, 0, fout), compact_out(fi, 0, fout)], axis=-2)
    c_b = jnp.concatenate([compact_out(fr, 1, bout), compact_out(fi, 1, bout)], axis=-2)

    kk = jnp.stack([
        jnp.einsum('ldgop,ldgpn->ldgon', f[k][0], bbar[0], precision=hp)
        - jnp.einsum('ldgop,ldgpn->ldgon', f[k][1], bbar[1], precision=hp)
        for k in range(m)])
    tau = jnp.arange(m)[None, :] - jnp.arange(m)[:, None]
    kf = kk[:, :, 0][jnp.clip(tau, 0, m - 1)]
    kb = kk[:, :, 1][jnp.clip(-tau, 0, m - 1)]
    tmask = lambda c: c[:, :, None, None, None, None]
    toe = jnp.where(tmask(tau >= 0), kf, 0.0) + jnp.where(tmask(tau <= 0), kb, 0.0)
    toe = toe.reshape(m, m, depth, n_oct, OCTET, n, n)
    a_tot = jnp.transpose(toe, (2, 3, 0, 4, 6, 1, 5)).reshape(depth, n_oct, m * OCTET * n, m * n)

    col = jnp.arange(OCTET * LANES)
    row = jnp.arange(LANES)
    r_a = (row[:, None] // n == col[None, :] // (OCTET * n)) & (row[:, None] % n == col[None, :] % n)
    r_b = (row[:, None] // p == col[None, :] // (OCTET * p)) & (row[:, None] % p == col[None, :] % p)

    lam8 = jnp.concatenate([pw[0][m].reshape(depth, 2, n_oct, 1, OCTET * p),
                            pw[1][m].reshape(depth, 2, n_oct, 1, OCTET * p)], axis=-1)
    lam8 = jnp.moveaxis(lam8, 1, 0)
    cast = lambda a: a.astype(BF16)
    return (cast(a_tot), cast(b_f), cast(b_b), cast(c_f), cast(c_b), cast(r_a), cast(r_b), lam8,
            n.bit_length() - 1, p.bit_length() - 1)


def _to_chunk_layout(u):
    b, t, w = u.shape
    n_oct = w // LANES
    u = u.reshape(b, t // S5_CHUNK, S5_CHUNK, n_oct, LANES)
    u = jnp.transpose(u, (3, 1, 0, 2, 4))
    return u.reshape(n_oct, (t // S5_CHUNK) * b, S5_CHUNK * LANES)


def _from_chunk_layout(y, b):
    n_oct, rows, _ = y.shape
    mc = rows // b
    y = y.reshape(n_oct, mc, b, S5_CHUNK, LANES)
    y = jnp.transpose(y, (2, 1, 3, 0, 4))
    return y.reshape(b, mc * S5_CHUNK, n_oct * LANES)


def _attn_kernel(*refs, group, hd, chunks):
    qt_ref = refs[0]
    ot_ref = refs[-1]
    kv = refs[1:-1]
    tq = qt_ref.shape[2]
    qt = jnp.concatenate([qt_ref[0, h * hd:(h + 1) * hd, :] for h in range(group)], axis=1)
    cols = group * tq
    m = jnp.full((1, cols), -jnp.inf, F32)
    acc = jnp.zeros((hd + BF16_ROWS, cols), F32)
    for i, tk in enumerate(chunks):
        k_ref, vt_ref = kv[2 * i], kv[2 * i + 1]
        ones = jnp.ones((BF16_ROWS, tk), BF16)
        for c in range(k_ref.shape[1] // tk):
            k = k_ref[0, c * tk:(c + 1) * tk, :]
            vt = jnp.concatenate([vt_ref[0, :, c * tk:(c + 1) * tk], ones], axis=0)
            st = _dot(k, qt)
            m_new = jnp.maximum(m, jnp.max(st, axis=0, keepdims=True))
            alpha = jnp.exp2(m - m_new)
            p = jnp.exp2(st - m_new)
            acc = alpha * acc + _dot(vt, p.astype(BF16))
            m = m_new
    ot = acc[:hd] / acc[hd:hd + 1]
    for h in range(group):
        ot_ref[0, h * hd:(h + 1) * hd, :] = ot[:, h * tq:(h + 1) * tq].astype(BF16)


def _attention(qt, kvs, hd, tq):
    b, aw, t = qt.shape
    group = aw // hd // N_KV_HEADS
    gw = group * hd
    chunks = tuple(min(k.shape[1], 512) for k, _ in kvs)
    in_specs = [pl.BlockSpec((1, gw, tq), lambda bi, j, i: (bi, j, i))]
    args = [qt]
    for k, vt in kvs:
        s = k.shape[1]
        in_specs.append(pl.BlockSpec((1, s, hd), lambda bi, j, i: (bi, 0, j)))
        in_specs.append(pl.BlockSpec((1, hd, s), lambda bi, j, i: (bi, j, 0)))
        args += [k, vt]
    return pl.pallas_call(
        functools.partial(_attn_kernel, group=group, hd=hd, chunks=chunks),
        grid=(b, N_KV_HEADS, t // tq),
        in_specs=in_specs,
        out_specs=pl.BlockSpec((1, gw, tq), lambda bi, j, i: (bi, j, i)),
        out_shape=jax.ShapeDtypeStruct((b, aw, t), BF16),
        compiler_params=_params(("arbitrary", "arbitrary", "arbitrary")),
        name="attention",
    )(*args)


def _outproj_kernel(zc_ref, zp_ref, zn_ref, y_ref, u_ref, at_ref, x_ref, mod_ref,
                    cwt_ref, d_ref, wglu_ref, bglu_ref, wo_ref, g_ref, o_ref, ubuf_ref, *, cw):
    i = pl.program_id(1)
    tm = x_ref.shape[1]

    def gated(z):
        z = z.astype(F32)
        return z[:, 2 * cw:] * z[:, :cw]

    zc = zc_ref[0].astype(F32)
    u = zc[:, 2 * cw:] * zc[:, :cw]
    prev = gated(zp_ref[0])[BF16_ROWS - 1:BF16_ROWS]
    nxt = gated(zn_ref[0])[0:1]
    ubuf_ref[SUBLANES - 1:SUBLANES, :] = jnp.where(i > 0, prev, 0.0)
    ubuf_ref[SUBLANES:SUBLANES + tm, :] = u
    ubuf_ref[SUBLANES + tm:SUBLANES + tm + 1, :] = jnp.where(i < pl.num_programs(1) - 1, nxt, 0.0)
    y = (cwt_ref[0, 0:1] * ubuf_ref[SUBLANES - 1:SUBLANES - 1 + tm, :] + cwt_ref[0, 1:2] * u
         + cwt_ref[0, 2:3] * ubuf_ref[SUBLANES + 1:SUBLANES + 1 + tm, :])
    conv = (zc[:, cw:2 * cw] * y).astype(BF16)

    ys = y_ref[0].astype(F32) + d_ref[0] * u_ref[0].astype(F32)
    gl = jax.nn.gelu(ys)
    gate = jax.nn.sigmoid(_dot(gl.astype(BF16), wglu_ref[0]) + bglu_ref[0])
    ssm = (gl * gate).astype(BF16)

    ns = conv.shape[1] + ssm.shape[1]
    tn = (((0,), (0,)), ((), ()))
    mix = (_dot(jnp.concatenate([conv, ssm], axis=1), wo_ref[0, 0:ns, :])
           + lax.dot_general(at_ref[0], wo_ref[0, ns:, :], tn, preferred_element_type=F32))
    o_ref[0] = x_ref[0] + mod_ref[0, 2:3, :] * _rms(mix, g_ref[0])


def _outproj(zc, y, u, at, x, mod, conv_w, ssm_d, w_glu, b_glu, w_out, gain, layer, dims, tm):
    b, t, d = x.shape
    cw, sw, aw, hd = dims
    nhalo = tm // BF16_ROWS
    last = t // BF16_ROWS - 1
    tok = lambda w: pl.BlockSpec((1, tm, w), lambda bi, i: (bi, i, 0))
    return pl.pallas_call(
        functools.partial(_outproj_kernel, cw=cw),
        grid=(b, t // tm),
        in_specs=[
            tok(3 * cw),
            pl.BlockSpec((1, BF16_ROWS, 3 * cw), lambda bi, i: (bi, jnp.maximum(i * nhalo - 1, 0), 0)),
            pl.BlockSpec((1, BF16_ROWS, 3 * cw), lambda bi, i: (bi, jnp.minimum((i + 1) * nhalo, last), 0)),
            tok(sw), tok(sw),
            pl.BlockSpec((1, aw, tm), lambda bi, i: (bi, 0, i)),
            tok(d),
            pl.BlockSpec((1, MOD_ROWS, d), lambda bi, i: (bi, 0, 0)),
            _layer_spec((conv_w.shape[1], cw), layer),
            _layer_spec((1, sw), layer),
            _layer_spec((sw, sw), layer),
            _layer_spec((1, sw), layer),
            _layer_spec((d, d), layer),
            _layer_spec((1, d), layer),
        ],
        out_specs=tok(d),
        out_shape=jax.ShapeDtypeStruct((b, t, d), F32),
        scratch_shapes=[pltpu.VMEM((tm + 2 * SUBLANES, cw), F32)],
        compiler_params=_params(("arbitrary", "arbitrary")),
        name="outproj",
    )(zc, zc, zc, y, u, at, x, mod, conv_w, ssm_d, w_glu, b_glu, w_out, gain)


def _ffn_kernel(x_ref, mod_ref, gpre_ref, gpost_ref, wg_ref, wu_ref, wd_ref, o_ref, h_ref, acc_ref, *, nj):
    j = pl.program_id(2)

    def hidden_tile(h):
        a = jax.nn.silu(_dot(h, wg_ref[0])) * _dot(h, wu_ref[0])
        return _dot(a.astype(BF16), wd_ref[0])

    def pre_norm():
        gain = gpre_ref[0] * (1.0 + mod_ref[0, 4:5, :])
        h = (_rms(x_ref[0], gain) + mod_ref[0, 3:4, :]).astype(BF16)
        h_ref[...] = h
        return h

    def post_norm(acc):
        o_ref[0] = x_ref[0] + mod_ref[0, 5:6, :] * _rms(acc, gpost_ref[0])

    if nj == 1:
        post_norm(hidden_tile(pre_norm()))
        return

    @pl.when(j == 0)
    def _():
        acc_ref[...] = hidden_tile(pre_norm())

    @pl.when((j > 0) & (j < nj - 1))
    def _():
        acc_ref[...] += hidden_tile(h_ref[...])

    @pl.when(j == nj - 1)
    def _():
        post_norm(acc_ref[...] + hidden_tile(h_ref[...]))


def _ffn(x, mod, g_pre, g_post, w_gate, w_up, w_down, layer, tm, th):
    b, t, d = x.shape
    f = w_gate.shape[-1]
    tok = pl.BlockSpec((1, tm, d), lambda bi, i, j: (bi, i, 0))
    return pl.pallas_call(
        functools.partial(_ffn_kernel, nj=f // th),
        grid=(b, t // tm, f // th),
        in_specs=[
            tok,
            pl.BlockSpec((1, MOD_ROWS, d), lambda bi, i, j: (bi, 0, 0)),
            _layer_spec((1, d), layer),
            _layer_spec((1, d), layer),
            pl.BlockSpec((1, d, th), lambda bi, i, j: (layer, 0, j)),
            pl.BlockSpec((1, d, th), lambda bi, i, j: (layer, 0, j)),
            pl.BlockSpec((1, th, d), lambda bi, i, j: (layer, j, 0)),
        ],
        out_specs=tok,
        out_shape=jax.ShapeDtypeStruct((b, t, d), F32),
        scratch_shapes=[pltpu.VMEM((tm, d), BF16), pltpu.VMEM((tm, d), F32)],
        input_output_aliases={0: 0},
        compiler_params=_params(("arbitrary", "arbitrary", "arbitrary")),
        name="ffn",
    )(x, mod, g_pre, g_post, w_gate, w_up, w_down)


def _rope_tables(t, hd):
    rows = t // GRID_W
    row = jnp.broadcast_to(jnp.arange(rows)[:, None], (rows, GRID_W)).reshape(-1).astype(F32)
    col = jnp.broadcast_to(jnp.arange(GRID_W)[None, :], (rows, GRID_W)).reshape(-1).astype(F32)
    half = hd // 2
    inv_freq = ROPE_THETA ** (-jnp.arange(0, half, 2, dtype=F32) / half)
    ar = row[:, None] * inv_freq
    ac = col[:, None] * inv_freq
    cos = jnp.concatenate([jnp.cos(ar), jnp.cos(ar), jnp.cos(ac), jnp.cos(ac)], axis=-1)
    sin = jnp.concatenate([-jnp.sin(ar), jnp.sin(ar), -jnp.sin(ac), jnp.sin(ac)], axis=-1)
    return cos, sin


def _tile(n, pref):
    return pref if n % pref == 0 else n


def kernel(x, c, ctx, c_ctx, w_mod, b_mod, g_pre_mix, g_post_mix, g_pre_ffn, g_post_ffn, w_in, conv_w, ssm_lam_re, ssm_lam_im, ssm_log_dt, ssm_b_re, ssm_b_im, ssm_c_re, ssm_c_im, ssm_d, w_glu, b_glu, q_norm, k_norm, w_out, w_gate, w_up, w_down):
    bsz, t, d = x.shape
    n_ctx = ctx.shape[1]
    depth = w_mod.shape[0]
    cw = conv_w.shape[-1]
    sw = w_glu.shape[-1]
    hd = q_norm.shape[-1]
    aw = d - cw - sw
    dims = (cw, sw, aw, hd)
    assert w_in.shape[-1] == 3 * cw + sw + aw + 2 * N_KV_HEADS * hd
    assert sw % LANES == 0 and t % (S5_CHUNK * BF16_ROWS) == 0 and n_ctx % (S5_CHUNK * BF16_ROWS) == 0

    cvec = jnp.concatenate([c, c_ctx[None, :], jnp.zeros((BF16_ROWS - bsz - 1, d), F32)], axis=0)
    mod = _modulation(cvec, w_mod, b_mod).reshape(depth, BF16_ROWS, N_MOD, d)
    mod = jnp.pad(mod, ((0, 0), (0, 0), (0, MOD_ROWS - N_MOD), (0, 0)))
    mod_x = mod[:, :bsz]
    mod_c = jnp.broadcast_to(mod[:, bsz:bsz + 1], (depth, bsz, MOD_ROWS, d))

    cos, sin = _rope_tables(t, hd)
    tabs = (cos, sin, cos.T, sin.T)
    ops = _s5_operators(ssm_lam_re, ssm_lam_im, ssm_log_dt, ssm_b_re, ssm_b_im, ssm_c_re, ssm_c_im)
    n_oct = sw // LANES

    r3 = lambda a: a.reshape(depth, 1, a.shape[-1])
    g_pre_mix, g_post_mix, g_pre_ffn, g_post_ffn = map(r3, (g_pre_mix, g_post_mix, g_pre_ffn, g_post_ffn))
    q_col = q_norm.reshape(depth, hd, 1)
    k_norm, ssm_d, b_glu = map(r3, (k_norm, ssm_d, b_glu))
    q_off = 3 * cw + sw
    k_off = q_off + aw
    v_off = k_off + N_KV_HEADS * hd
    w_a = w_in.astype(BF16)
    assert q_off % LANES == 0 and v_off % LANES == 0 and aw % LANES == 0
    qv_blocks = ([q_off // LANES + i for i in range(aw // LANES)]
                 + [v_off // LANES + i for i in range(N_KV_HEADS * hd // LANES)])
    w_bt = _transposed_columns(w_in, qv_blocks, LANES)
    w_glu, w_out, w_gate, w_up, w_down = (w.astype(BF16) for w in (w_glu, w_out, w_gate, w_up, w_down))

    tm_x, tm_c = _tile(t, 512), _tile(n_ctx, 256)
    tq_x, tq_c = _tile(t, 1024), _tile(n_ctx, 256)
    f = w_gate.shape[-1]
    th = max(c for c in range(LANES, 512 + 1, LANES) if f % c == 0)
    cb_x = _tile(t // S5_CHUNK, 128)
    cb_c = n_ctx // S5_CHUNK
    h_zero = jnp.zeros((2, n_oct, bsz, 2 * OCTET * ssm_lam_re.shape[-1]), F32)

    def s5(zs, h0, layer, cb):
        y, hfin = _s5(_to_chunk_layout(zs), ops, h0, layer, bsz, cb)
        return _from_chunk_layout(y, bsz), hfin

    xc = ctx
    for l in range(depth):
        want_ctx = l < depth - 1
        zc_x, zs_x, q_x, k_x, v_x = _inproj(x, mod_x[l], g_pre_mix, w_a, w_bt, q_col, k_norm, tabs, l, dims, True, tm_x)
        zc_c, zs_c, q_c, k_c, v_c = _inproj(xc, mod_c[l], g_pre_mix, w_a, w_bt, q_col, k_norm, tabs, l, dims, False, tm_c)
        y_c, h_ctx = s5(zs_c, h_zero, l, cb_c)
        y_x, _ = s5(zs_x, h_ctx, l, cb_x)
        at_x = _attention(q_x, [(k_x, v_x), (k_c, v_c)], hd, tq_x)
        x = _outproj(zc_x, y_x, zs_x, at_x, x, mod_x[l], conv_w, ssm_d, w_glu, b_glu, w_out,
                     g_post_mix, l, dims, tm_x)
        x = _ffn(x, mod_x[l], g_pre_ffn, g_post_ffn, w_gate, w_up, w_down, l, tm_x, th)
        if want_ctx:
            at_c = _attention(q_c, [(k_c, v_c)], hd, tq_c)
            xc = _outproj(zc_c, y_c, zs_c, at_c, xc, mod_c[l], conv_w, ssm_d, w_glu, b_glu, w_out,
                          g_post_mix, l, dims, tm_c)
            xc = _ffn(xc, mod_c[l], g_pre_ffn, g_post_ffn, w_gate, w_up, w_down, l, tm_c, th)
    return x
```

```python
import functools
import math

import jax
import jax.numpy as jnp
from jax import lax
from jax.experimental import pallas as pl
from jax.experimental.pallas import tpu as pltpu

F32 = jnp.float32
BF16 = jnp.bfloat16

GRID_W = 64
N_KV_HEADS = 2
N_MOD = 6
ROPE_THETA = 10000.0
RMS_EPS = 1e-6

LANES = 128
SUBLANES = 8
BF16_ROWS = 16
V7X_VMEM_LIMIT_BYTES = 56 * 1024 * 1024

S5_CHUNK = 8
OCTET = LANES // 16

MOD_ROWS = 8

def _dot(a, b):
    return jnp.dot(a, b, preferred_element_type=F32)


def _rms(x, gain):
    ms = jnp.mean(x * x, axis=-1, keepdims=True)
    return x * lax.rsqrt(ms + RMS_EPS) * gain


def _const_spec(shape):
    nd = len(shape)
    return pl.BlockSpec(shape, lambda *_: (0,) * nd, pipeline_mode=pl.Buffered(1))


def _layer_spec(shape, layer):
    nd = len(shape)
    return pl.BlockSpec((1,) + shape, lambda *_: (layer,) + (0,) * nd, pipeline_mode=pl.Buffered(1))


def _params(semantics):
    return pltpu.CompilerParams(dimension_semantics=semantics, vmem_limit_bytes=V7X_VMEM_LIMIT_BYTES)


def _mod_kernel(c_ref, w_ref, b_ref, o_ref):
    s = jax.nn.silu(c_ref[...]).astype(BF16)
    o_ref[0] = _dot(s, w_ref[0].astype(BF16)) + b_ref[0]


def _modulation(cvec, w_mod, b_mod):
    depth, d, n = w_mod.shape
    r = cvec.shape[0]
    tn = min(n, 1024)
    return pl.pallas_call(
        _mod_kernel,
        grid=(depth, n // tn),
        in_specs=[
            pl.BlockSpec((r, d), lambda l, j: (0, 0)),
            pl.BlockSpec((1, d, tn), lambda l, j: (l, 0, j)),
            pl.BlockSpec((1, 1, tn), lambda l, j: (l, 0, j)),
        ],
        out_specs=pl.BlockSpec((1, r, tn), lambda l, j: (l, 0, j)),
        out_shape=jax.ShapeDtypeStruct((depth, r, n), F32),
        compiler_params=_params(("arbitrary", "arbitrary")),
        name="modulation",
    )(cvec, w_mod, b_mod.reshape(depth, 1, n))


def _wt_kernel(blk_ref, w_ref, o_ref):
    del blk_ref
    o_ref[0] = w_ref[0].T.astype(BF16)


def _transposed_columns(w, col_blocks, tn):
    depth, d, _ = w.shape
    blocks = jnp.asarray(col_blocks, jnp.int32)
    return pl.pallas_call(
        _wt_kernel,
        grid_spec=pltpu.PrefetchScalarGridSpec(
            num_scalar_prefetch=1,
            grid=(depth, len(col_blocks)),
            in_specs=[pl.BlockSpec((1, d, tn), lambda l, j, blk: (l, 0, blk[j]))],
            out_specs=pl.BlockSpec((1, tn, d), lambda l, j, blk: (l, j, 0)),
        ),
        out_shape=jax.ShapeDtypeStruct((depth, len(col_blocks) * tn, d), BF16),
        compiler_params=_params(("arbitrary", "arbitrary")),
        name="transpose_weights",
    )(blocks, w)


def _swap32(y, lane_lo):
    return jnp.where(lane_lo, pltpu.roll(y, LANES - 32, 1), pltpu.roll(y, 32, 1))


def _swap32_rows(y):
    return jnp.concatenate([y[32:64], y[0:32], y[96:128], y[64:96]], axis=0)


def _inproj_kernel(x_ref, mod_ref, g_ref, wa_ref, wbt_ref, qg_ref, kg_ref, cos_ref, sin_ref, cost_ref, sint_ref,
                   zc_ref, zs_ref, qt_ref, k_ref, vt_ref, *, cw, sw, aw, hd, rope):
    x = x_ref[0]
    h = _rms(x, g_ref[0] * (1.0 + mod_ref[0, 1:2, :])) + mod_ref[0, 0:1, :]
    h = h.astype(BF16)
    kvw = N_KV_HEADS * hd
    k_off = 3 * cw + sw
    zc_ref[0] = _dot(h, wa_ref[0, :, 0:3 * cw]).astype(BF16)
    zs_ref[0] = _dot(h, wa_ref[0, :, 3 * cw:k_off]).astype(BF16)

    tm = x.shape[0]
    nt = (((1,), (1,)), ((), ()))
    vt_ref[0] = lax.dot_general(wbt_ref[0, aw:aw + kvw, :], h, nt, preferred_element_type=F32).astype(BF16)

    lane_lo = (lax.broadcasted_iota(jnp.int32, (tm, hd), 1) & 32) == 0
    k = _dot(h, wa_ref[0, :, k_off + aw:k_off + aw + kvw])
    for i in range(N_KV_HEADS):
        y = _rms(k[:, i * hd:(i + 1) * hd], kg_ref[0])
        if rope:
            y = y * cos_ref[...] + _swap32(y, lane_lo) * sin_ref[...]
        k_ref[0, :, i * hd:(i + 1) * hd] = y.astype(BF16)

    qscale = hd ** -0.5 * math.log2(math.e)
    qt = lax.dot_general(wbt_ref[0, 0:aw, :], h, nt, preferred_element_type=F32)
    for i in range(aw // hd):
        z = qt[i * hd:(i + 1) * hd, :]
        ms = jnp.mean(z * z, axis=0, keepdims=True)
        y = z * lax.rsqrt(ms + RMS_EPS) * qg_ref[0]
        if rope:
            y = y * cost_ref[...] + _swap32_rows(y) * sint_ref[...]
        qt_ref[0, i * hd:(i + 1) * hd, :] = (y * qscale).astype(BF16)


def _inproj(x, mod, gain, w_a, w_bt, q_gain_col, k_gain, tabs, layer, dims, rope, tm):
    b, t, d = x.shape
    cw, sw, aw, hd = dims
    kvw = N_KV_HEADS * hd
    cos, sin, cos_t, sin_t = tabs
    tok = lambda w: pl.BlockSpec((1, tm, w), lambda bi, i: (bi, i, 0))
    tok_t = lambda w: pl.BlockSpec((1, w, tm), lambda bi, i: (bi, 0, i))
    tab = pl.BlockSpec((tm, hd), lambda bi, i: (i, 0))
    tab_t = pl.BlockSpec((hd, tm), lambda bi, i: (0, i))
    sds = jax.ShapeDtypeStruct
    return pl.pallas_call(
        functools.partial(_inproj_kernel, cw=cw, sw=sw, aw=aw, hd=hd, rope=rope),
        grid=(b, t // tm),
        in_specs=[
            tok(d),
            pl.BlockSpec((1, MOD_ROWS, d), lambda bi, i: (bi, 0, 0)),
            _layer_spec((1, d), layer),
            _layer_spec((d, w_a.shape[-1]), layer),
            _layer_spec((aw + kvw, d), layer),
            _layer_spec((hd, 1), layer),
            _layer_spec((1, hd), layer),
            tab, tab, tab_t, tab_t,
        ],
        out_specs=[tok(3 * cw), tok(sw), tok_t(aw), tok(kvw), tok_t(kvw)],
        out_shape=[sds((b, t, 3 * cw), BF16), sds((b, t, sw), BF16), sds((b, aw, t), BF16),
                   sds((b, t, kvw), BF16), sds((b, kvw, t), BF16)],
        compiler_params=_params(("arbitrary", "arbitrary")),
        name="inproj",
    )(x, mod, gain, w_a, w_bt, q_gain_col, k_gain, cos, sin, cos_t, sin_t)


def _s5_kernel(x_ref, atc_ref, bfc_ref, bbc_ref, cfc_ref, cbc_ref, ra_ref, rb_ref, lam_ref, h0_ref,
               y_ref, hfin_ref,
               at_ref, bf_ref, bb_ref, cf_ref, cb_ref, s_ref, hb_all_ref, h_ref,
               *, cb, nb, nblk, n_shift, p_shift):
    s = pl.program_id(1)
    half = h_ref.shape[-1] // 2
    blk = cb * nb

    def expand(comp_ref, r_ref, row_shift, col_shift):
        full = _dot(comp_ref[0, 0], r_ref[...])
        rg = (lax.broadcasted_iota(jnp.int32, full.shape, 0) >> row_shift) & (OCTET - 1)
        cg = (lax.broadcasted_iota(jnp.int32, full.shape, 1) >> col_shift) & (OCTET - 1)
        return jnp.where(rg == cg, full, 0.0).astype(at_ref.dtype)

    @pl.when(s == 0)
    def _():
        at_ref[...] = expand(atc_ref, ra_ref, n_shift, n_shift)
        bf_ref[...] = expand(bfc_ref, rb_ref, n_shift, p_shift)
        bb_ref[...] = expand(bbc_ref, rb_ref, n_shift, p_shift)
        cf_ref[...] = expand(cfc_ref, ra_ref, p_shift, n_shift)
        cb_ref[...] = expand(cbc_ref, ra_ref, p_shift, n_shift)
        h_ref[...] = h0_ref[1, 0]

    def scan(lam, reverse):
        lr = jnp.broadcast_to(lam[:, :half], (nb, half))
        li = jnp.broadcast_to(lam[:, half:], (nb, half))

        def body(k, carry):
            hr, hi = carry
            c = (cb - 1 - k) if reverse else k
            r0 = pl.multiple_of(c * nb, nb)
            srow = s_ref[pl.ds(r0, nb), :]
            s_ref[pl.ds(r0, nb), :] = jnp.concatenate([hr, hi], axis=1)
            nhr = lr * hr - li * hi + srow[:, :half]
            nhi = lr * hi + li * hr + srow[:, half:]
            return nhr, nhi

        hr, hi = lax.fori_loop(0, cb, body, (h_ref[:, :half], h_ref[:, half:]), unroll=2)
        h_ref[...] = jnp.concatenate([hr, hi], axis=1)

    @pl.when(s < nblk)
    def _():
        s_ref[...] = _dot(x_ref[0], bb_ref[...])
        scan(lam_ref[1, 0, 0], True)
        r0 = pl.multiple_of((nblk - 1 - s) * blk, blk)
        hb_all_ref[pl.ds(r0, blk), :] = s_ref[...].astype(hb_all_ref.dtype)

    @pl.when(s == nblk - 1)
    def _():
        hfin_ref[1, 0] = h_ref[...]
        h_ref[...] = h0_ref[0, 0]

    @pl.when(s >= nblk)
    def _():
        x = x_ref[0]
        s_ref[...] = _dot(x, bf_ref[...])
        scan(lam_ref[0, 0, 0], False)
        r0 = pl.multiple_of((s - nblk) * blk, blk)
        y = (_dot(x, at_ref[...]) + _dot(s_ref[...].astype(cf_ref.dtype), cf_ref[...])
             + _dot(hb_all_ref[pl.ds(r0, blk), :], cb_ref[...]))
        y_ref[0] = y.astype(y_ref.dtype)

    @pl.when(s == 2 * nblk - 1)
    def _():
        hfin_ref[0, 0] = h_ref[...]


def _s5(xq, ops, h0, layer, nb, cb):
    at_c, bf_c, bb_c, cf_c, cb_c, r_a, r_b, lam8, n_shift, p_shift = ops
    n_oct, rows, kx = xq.shape
    sw = lam8.shape[-1]
    blk = cb * nb
    nblk = rows // blk
    xblk = lambda o, s: (o, jnp.where(s < nblk, nblk - 1 - s, s - nblk), 0)
    yblk = lambda o, s: (o, jnp.maximum(s - nblk, 0), 0)
    comp = lambda a: pl.BlockSpec((1, 1) + a.shape[2:], lambda o, s: (layer, o, 0, 0))
    whole = lambda a: pl.BlockSpec(a.shape, lambda o, s: (0, 0))
    dt = xq.dtype
    return pl.pallas_call(
        functools.partial(_s5_kernel, cb=cb, nb=nb, nblk=nblk, n_shift=n_shift, p_shift=p_shift),
        grid=(n_oct, 2 * nblk),
        in_specs=[
            pl.BlockSpec((1, blk, kx), xblk),
            comp(at_c), comp(bf_c), comp(bb_c), comp(cf_c), comp(cb_c), whole(r_a), whole(r_b),
            pl.BlockSpec((2, 1, 1, 1, sw), lambda o, s: (0, layer, o, 0, 0)),
            pl.BlockSpec((2, 1, nb, sw), lambda o, s: (0, o, 0, 0)),
        ],
        out_specs=[
            pl.BlockSpec((1, blk, kx), yblk),
            pl.BlockSpec((2, 1, nb, sw), lambda o, s: (0, o, 0, 0)),
        ],
        out_shape=[
            jax.ShapeDtypeStruct((n_oct, rows, kx), dt),
            jax.ShapeDtypeStruct((2, n_oct, nb, sw), F32),
        ],
        scratch_shapes=[
            pltpu.VMEM((kx, kx), dt), pltpu.VMEM((kx, sw), dt), pltpu.VMEM((kx, sw), dt),
            pltpu.VMEM((sw, kx), dt), pltpu.VMEM((sw, kx), dt),
            pltpu.VMEM((blk, sw), F32), pltpu.VMEM((rows, sw), dt), pltpu.VMEM((nb, sw), F32),
        ],
        compiler_params=_params(("arbitrary", "arbitrary")),
        name="s5_mixer",
    )(xq, at_c, bf_c, bb_c, cf_c, cb_c, r_a, r_b, lam8, h0)


def _cmul(a, b):
    return a[0] * b[0] - a[1] * b[1], a[0] * b[1] + a[1] * b[0]


def _s5_operators(lam_re, lam_im, log_dt, b_re, b_im, c_re, c_im):
    hp = lax.Precision.HIGHEST
    depth, _, g, p = lam_re.shape
    n = b_re.shape[-1]
    n_oct = g // OCTET
    m = S5_CHUNK
    assert n & (n - 1) == 0 and p & (p - 1) == 0 and m * n == LANES and 2 * p == LANES
    dt = jnp.exp(log_dt)[..., None]
    mag = jnp.exp(lam_re * dt)
    lam_bar = (mag * jnp.cos(lam_im * dt), mag * jnp.sin(lam_im * dt))
    den = lam_re * lam_re + lam_im * lam_im
    num = (lam_bar[0] - 1.0, lam_bar[1])
    ratio = ((num[0] * lam_re + num[1] * lam_im) / den, (num[1] * lam_re - num[0] * lam_im) / den)
    bbar = _cmul((ratio[0][..., None], ratio[1][..., None]), (b_re, b_im))
    pows = [(jnp.ones_like(lam_re), jnp.zeros_like(lam_re))]
    for _ in range(m):
        pows.append(_cmul(pows[-1], lam_bar))
    pw = (jnp.stack([q[0] for q in pows]), jnp.stack([q[1] for q in pows]))

    def compact_in(e, d, exps):
        sel = jnp.stack([e[k][:, d] for k in exps]).reshape(m, depth, n_oct, OCTET, p, n)
        return jnp.transpose(sel, (1, 2, 0, 3, 5, 4)).reshape(depth, n_oct, m * OCTET * n, p)

    def compact_out(f, d, exps):
        sel = jnp.stack([f[k][:, d] for k in exps]).reshape(m, depth, n_oct, OCTET, n, p)
        return jnp.transpose(sel, (1, 2, 3, 5, 0, 4)).reshape(depth, n_oct, OCTET * p, m * n)

    e = [_cmul((pw[0][k][..., None], pw[1][k][..., None]), bbar) for k in range(m + 1)]
    f = [_cmul((c_re, c_im), (pw[0][k][:, :, :, None, :], pw[1][k][:, :, :, None, :])) for k in range(m + 1)]
    er, ei = [q[0] for q in e], [q[1] for q in e]
    fr, fi = [q[0] for q in f], [-q[1] for q in f]
    fexp = [m - 1 - i for i in range(m)]
    bexp = list(range(m))
    fout = [j + 1 for j in range(m)]
    bout = [m - j for j in range(m)]
    b_f = jnp.concatenate([compact_in(er, 0, fexp), compact_in(ei, 0, fexp)], axis=-1)
    b_b = jnp.concatenate([compact_in(er, 1, bexp), compact_in(ei, 1, bexp)], axis=-1)
    c_f = jnp.concatenate([compact_out(fr, 0, fout), compact_out(fi, 0, fout)], axis=-2)
    c_b = jnp.concatenate([compact_out(fr, 1, bout), compact_out(fi, 1, bout)], axis=-2)

    kk = jnp.stack([
        jnp.einsum('ldgop,ldgpn->ldgon', f[k][0], bbar[0], precision=hp)
        - jnp.einsum('ldgop,ldgpn->ldgon', f[k][1], bbar[1], precision=hp)
        for k in range(m)])
    tau = jnp.arange(m)[None, :] - jnp.arange(m)[:, None]
    kf = kk[:, :, 0][jnp.clip(tau, 0, m - 1)]
    kb = kk[:, :, 1][jnp.clip(-tau, 0, m - 1)]
    tmask = lambda c: c[:, :, None, None, None, None]
    toe = jnp.where(tmask(tau >= 0), kf, 0.0) + jnp.where(tmask(tau <= 0), kb, 0.0)
    toe = toe.reshape(m, m, depth, n_oct, OCTET, n, n)
    a_tot = jnp.transpose(toe, (2, 3, 0, 4, 6, 1, 5)).reshape(depth, n_oct, m * OCTET * n, m * n)

    col = jnp.arange(OCTET * LANES)
    row = jnp.arange(LANES)
    r_a = (row[:, None] // n == col[None, :] // (OCTET * n)) & (row[:, None] % n == col[None, :] % n)
    r_b = (row[:, None] // p == col[None, :] // (OCTET * p)) & (row[:, None] % p == col[None, :] % p)

    lam8 = jnp.concatenate([pw[0][m].reshape(depth, 2, n_oct, 1, OCTET * p),
                            pw[1][m].reshape(depth, 2, n_oct, 1, OCTET * p)], axis=-1)
    lam8 = jnp.moveaxis(lam8, 1, 0)
    cast = lambda a: a.astype(BF16)
    return (cast(a_tot), cast(b_f), cast(b_b), cast(c_f), cast(c_b), cast(r_a), cast(r_b), lam8,
            n.bit_length() - 1, p.bit_length() - 1)


def _to_chunk_layout(u):
    b, t, w = u.shape
    n_oct = w // LANES
    u = u.reshape(b, t // S5_CHUNK, S5_CHUNK, n_oct, LANES)
    u = jnp.transpose(u, (3, 1, 0, 2, 4))
    return u.reshape(n_oct, (t // S5_CHUNK) * b, S5_CHUNK * LANES)


def _from_chunk_layout(y, b):
    n_oct, rows, _ = y.shape
    mc = rows // b
    y = y.reshape(n_oct, mc, b, S5_CHUNK, LANES)
    y = jnp.transpose(y, (2, 1, 3, 0, 4))
    return y.reshape(b, mc * S5_CHUNK, n_oct * LANES)


def _attn_kernel(*refs, group, hd, chunks):
    qt_ref = refs[0]
    ot_ref = refs[-1]
    kv = refs[1:-1]
    tq = qt_ref.shape[2]
    qt = jnp.concatenate([qt_ref[0, h * hd:(h + 1) * hd, :] for h in range(group)], axis=1)
    cols = group * tq
    m = jnp.full((1, cols), -jnp.inf, F32)
    acc = jnp.zeros((hd + BF16_ROWS, cols), F32)
    for i, tk in enumerate(chunks):
        k_ref, vt_ref = kv[2 * i], kv[2 * i + 1]
        ones = jnp.ones((BF16_ROWS, tk), BF16)
        for c in range(k_ref.shape[1] // tk):
            k = k_ref[0, c * tk:(c + 1) * tk, :]
            vt = jnp.concatenate([vt_ref[0, :, c * tk:(c + 1) * tk], ones], axis=0)
            st = _dot(k, qt)
            m_new = jnp.maximum(m, jnp.max(st, axis=0, keepdims=True))
            alpha = jnp.exp2(m - m_new)
            p = jnp.exp2(st - m_new)
            acc = alpha * acc + _dot(vt, p.astype(BF16))
            m = m_new
    ot = acc[:hd] / acc[hd:hd + 1]
    for h in range(group):
        ot_ref[0, h * hd:(h + 1) * hd, :] = ot[:, h * tq:(h + 1) * tq].astype(BF16)


def _attention(qt, kvs, hd, tq):
    b, aw, t = qt.shape
    group = aw // hd // N_KV_HEADS
    gw = group * hd
    chunks = tuple(min(k.shape[1], 512) for k, _ in kvs)
    in_specs = [pl.BlockSpec((1, gw, tq), lambda bi, j, i: (bi, j, i))]
    args = [qt]
    for k, vt in kvs:
        s = k.shape[1]
        in_specs.append(pl.BlockSpec((1, s, hd), lambda bi, j, i: (bi, 0, j)))
        in_specs.append(pl.BlockSpec((1, hd, s), lambda bi, j, i: (bi, j, 0)))
        args += [k, vt]
    return pl.pallas_call(
        functools.partial(_attn_kernel, group=group, hd=hd, chunks=chunks),
        grid=(b, N_KV_HEADS, t // tq),
        in_specs=in_specs,
        out_specs=pl.BlockSpec((1, gw, tq), lambda bi, j, i: (bi, j, i)),
        out_shape=jax.ShapeDtypeStruct((b, aw, t), BF16),
        compiler_params=_params(("arbitrary", "arbitrary", "arbitrary")),
        name="attention",
    )(*args)


def _outproj_kernel(zc_ref, zp_ref, zn_ref, y_ref, u_ref, at_ref, x_ref, mod_ref,
                    cwt_ref, d_ref, wglu_ref, bglu_ref, wo_ref, g_ref, o_ref, ubuf_ref, *, cw):
    i = pl.program_id(1)
    tm = x_ref.shape[1]

    def gated(z):
        z = z.astype(F32)
        return z[:, 2 * cw:] * z[:, :cw]

    zc = zc_ref[0].astype(F32)
    u = zc[:, 2 * cw:] * zc[:, :cw]
    prev = gated(zp_ref[0])[BF16_ROWS - 1:BF16_ROWS]
    nxt = gated(zn_ref[0])[0:1]
    ubuf_ref[SUBLANES - 1:SUBLANES, :] = jnp.where(i > 0, prev, 0.0)
    ubuf_ref[SUBLANES:SUBLANES + tm, :] = u
    ubuf_ref[SUBLANES + tm:SUBLANES + tm + 1, :] = jnp.where(i < pl.num_programs(1) - 1, nxt, 0.0)
    y = (cwt_ref[0, 0:1] * ubuf_ref[SUBLANES - 1:SUBLANES - 1 + tm, :] + cwt_ref[0, 1:2] * u
         + cwt_ref[0, 2:3] * ubuf_ref[SUBLANES + 1:SUBLANES + 1 + tm, :])
    conv = (zc[:, cw:2 * cw] * y).astype(BF16)

    ys = y_ref[0].astype(F32) + d_ref[0] * u_ref[0].astype(F32)
    gl = jax.nn.gelu(ys)
    gate = jax.nn.sigmoid(_dot(gl.astype(BF16), wglu_ref[0]) + bglu_ref[0])
    ssm = (gl * gate).astype(BF16)

    ns = conv.shape[1] + ssm.shape[1]
    tn = (((0,), (0,)), ((), ()))
    mix = (_dot(jnp.concatenate([conv, ssm], axis=1), wo_ref[0, 0:ns, :])
           + lax.dot_general(at_ref[0], wo_ref[0, ns:, :], tn, preferred_element_type=F32))
    o_ref[0] = x_ref[0] + mod_ref[0, 2:3, :] * _rms(mix, g_ref[0])


def _outproj(zc, y, u, at, x, mod, conv_w, ssm_d, w_glu, b_glu, w_out, gain, layer, dims, tm):
    b, t, d = x.shape
    cw, sw, aw, hd = dims
    nhalo = tm // BF16_ROWS
    last = t // BF16_ROWS - 1
    tok = lambda w: pl.BlockSpec((1, tm, w), lambda bi, i: (bi, i, 0))
    return pl.pallas_call(
        functools.partial(_outproj_kernel, cw=cw),
        grid=(b, t // tm),
        in_specs=[
            tok(3 * cw),
            pl.BlockSpec((1, BF16_ROWS, 3 * cw), lambda bi, i: (bi, jnp.maximum(i * nhalo - 1, 0), 0)),
            pl.BlockSpec((1, BF16_ROWS, 3 * cw), lambda bi, i: (bi, jnp.minimum((i + 1) * nhalo, last), 0)),
            tok(sw), tok(sw),
            pl.BlockSpec((1, aw, tm), lambda bi, i: (bi, 0, i)),
            tok(d),
            pl.BlockSpec((1, MOD_ROWS, d), lambda bi, i: (bi, 0, 0)),
            _layer_spec((conv_w.shape[1], cw), layer),
            _layer_spec((1, sw), layer),
            _layer_spec((sw, sw), layer),
            _layer_spec((1, sw), layer),
            _layer_spec((d, d), layer),
            _layer_spec((1, d), layer),
        ],
        out_specs=tok(d),
        out_shape=jax.ShapeDtypeStruct((b, t, d), F32),
        scratch_shapes=[pltpu.VMEM((tm + 2 * SUBLANES, cw), F32)],
        compiler_params=_params(("arbitrary", "arbitrary")),
        name="outproj",
    )(zc, zc, zc, y, u, at, x, mod, conv_w, ssm_d, w_glu, b_glu, w_out, gain)


def _ffn_kernel(x_ref, mod_ref, gpre_ref, gpost_ref, wga_ref, wua_ref, wda_ref, wgb_ref, wub_ref, wdb_ref,
                o_ref, h_ref, acc_ref, *, nh):
    j = pl.program_id(2)
    nj = (nh + 1) // 2

    def hidden_tile(h, wg_ref, wu_ref, wd_ref):
        a = jax.nn.silu(_dot(h, wg_ref[0])) * _dot(h, wu_ref[0])
        return _dot(a.astype(BF16), wd_ref[0])

    def hidden_pair(h, both=True):
        out = hidden_tile(h, wga_ref, wua_ref, wda_ref)
        if both:
            out = out + hidden_tile(h, wgb_ref, wub_ref, wdb_ref)
        return out

    def pre_norm():
        gain = gpre_ref[0] * (1.0 + mod_ref[0, 4:5, :])
        h = (_rms(x_ref[0], gain) + mod_ref[0, 3:4, :]).astype(BF16)
        h_ref[...] = h
        return h

    def post_norm(acc):
        o_ref[0] = x_ref[0] + mod_ref[0, 5:6, :] * _rms(acc, gpost_ref[0])

    last_both = nh % 2 == 0
    if nj == 1:
        post_norm(hidden_pair(pre_norm(), last_both))
        return

    @pl.when(j == 0)
    def _():
        acc_ref[...] = hidden_pair(pre_norm())

    @pl.when((j > 0) & (j < nj - 1))
    def _():
        acc_ref[...] += hidden_pair(h_ref[...])

    @pl.when(j == nj - 1)
    def _():
        post_norm(acc_ref[...] + hidden_pair(h_ref[...], last_both))


def _ffn(x, mod, g_pre, g_post, w_gate, w_up, w_down, layer, tm, th):
    b, t, d = x.shape
    f = w_gate.shape[-1]
    nh = f // th
    tok = pl.BlockSpec((1, tm, d), lambda bi, i, j: (bi, i, 0))
    cols = lambda k: pl.BlockSpec((1, d, th), lambda bi, i, j: (layer, 0, jnp.minimum(2 * j + k, nh - 1)))
    rows = lambda k: pl.BlockSpec((1, th, d), lambda bi, i, j: (layer, jnp.minimum(2 * j + k, nh - 1), 0))
    return pl.pallas_call(
        functools.partial(_ffn_kernel, nh=nh),
        grid=(b, t // tm, (nh + 1) // 2),
        in_specs=[
            tok,
            pl.BlockSpec((1, MOD_ROWS, d), lambda bi, i, j: (bi, 0, 0)),
            _layer_spec((1, d), layer),
            _layer_spec((1, d), layer),
            cols(0), cols(0), rows(0), cols(1), cols(1), rows(1),
        ],
        out_specs=tok,
        out_shape=jax.ShapeDtypeStruct((b, t, d), F32),
        scratch_shapes=[pltpu.VMEM((tm, d), BF16), pltpu.VMEM((tm, d), F32)],
        input_output_aliases={0: 0},
        compiler_params=_params(("arbitrary", "arbitrary", "arbitrary")),
        name="ffn",
    )(x, mod, g_pre, g_post, w_gate, w_up, w_down, w_gate, w_up, w_down)


def _rope_tables(t, hd):
    rows = t // GRID_W
    row = jnp.broadcast_to(jnp.arange(rows)[:, None], (rows, GRID_W)).reshape(-1).astype(F32)
    col = jnp.broadcast_to(jnp.arange(GRID_W)[None, :], (rows, GRID_W)).reshape(-1).astype(F32)
    half = hd // 2
    inv_freq = ROPE_THETA ** (-jnp.arange(0, half, 2, dtype=F32) / half)
    ar = row[:, None] * inv_freq
    ac = col[:, None] * inv_freq
    cos = jnp.concatenate([jnp.cos(ar), jnp.cos(ar), jnp.cos(ac), jnp.cos(ac)], axis=-1)
    sin = jnp.concatenate([-jnp.sin(ar), jnp.sin(ar), -jnp.sin(ac), jnp.sin(ac)], axis=-1)
    return cos, sin


def _tile(n, pref):
    return pref if n % pref == 0 else n


def kernel(x, c, ctx, c_ctx, w_mod, b_mod, g_pre_mix, g_post_mix, g_pre_ffn, g_post_ffn, w_in, conv_w, ssm_lam_re, ssm_lam_im, ssm_log_dt, ssm_b_re, ssm_b_im, ssm_c_re, ssm_c_im, ssm_d, w_glu, b_glu, q_norm, k_norm, w_out, w_gate, w_up, w_down):
    bsz, t, d = x.shape
    n_ctx = ctx.shape[1]
    depth = w_mod.shape[0]
    cw = conv_w.shape[-1]
    sw = w_glu.shape[-1]
    hd = q_norm.shape[-1]
    aw = d - cw - sw
    dims = (cw, sw, aw, hd)
    assert w_in.shape[-1] == 3 * cw + sw + aw + 2 * N_KV_HEADS * hd
    assert sw % LANES == 0 and t % (S5_CHUNK * BF16_ROWS) == 0 and n_ctx % (S5_CHUNK * BF16_ROWS) == 0

    cvec = jnp.concatenate([c, c_ctx[None, :], jnp.zeros((BF16_ROWS - bsz - 1, d), F32)], axis=0)
    mod = _modulation(cvec, w_mod, b_mod).reshape(depth, BF16_ROWS, N_MOD, d)
    mod = jnp.pad(mod, ((0, 0), (0, 0), (0, MOD_ROWS - N_MOD), (0, 0)))
    mod_x = mod[:, :bsz]
    mod_c = jnp.broadcast_to(mod[:, bsz:bsz + 1], (depth, bsz, MOD_ROWS, d))

    cos, sin = _rope_tables(t, hd)
    tabs = (cos, sin, cos.T, sin.T)
    ops = _s5_operators(ssm_lam_re, ssm_lam_im, ssm_log_dt, ssm_b_re, ssm_b_im, ssm_c_re, ssm_c_im)
    n_oct = sw // LANES

    r3 = lambda a: a.reshape(depth, 1, a.shape[-1])
    g_pre_mix, g_post_mix, g_pre_ffn, g_post_ffn = map(r3, (g_pre_mix, g_post_mix, g_pre_ffn, g_post_ffn))
    q_col = q_norm.reshape(depth, hd, 1)
    k_norm, ssm_d, b_glu = map(r3, (k_norm, ssm_d, b_glu))
    q_off = 3 * cw + sw
    k_off = q_off + aw
    v_off = k_off + N_KV_HEADS * hd
    w_a = w_in.astype(BF16)
    assert q_off % LANES == 0 and v_off % LANES == 0 and aw % LANES == 0
    qv_blocks = ([q_off // LANES + i for i in range(aw // LANES)]
                 + [v_off // LANES + i for i in range(N_KV_HEADS * hd // LANES)])
    w_bt = _transposed_columns(w_in, qv_blocks, LANES)
    w_glu, w_out, w_gate, w_up, w_down = (w.astype(BF16) for w in (w_glu, w_out, w_gate, w_up, w_down))

    tm_x, tm_c = _tile(t, 512), _tile(n_ctx, 256)
    tq_x, tq_c = _tile(t, 1024), _tile(n_ctx, 256)
    f = w_gate.shape[-1]
    th = max(c for c in range(LANES, 512 + 1, LANES) if f % c == 0)
    cb_x = _tile(t // S5_CHUNK, 128)
    cb_c = n_ctx // S5_CHUNK
    h_zero = jnp.zeros((2, n_oct, bsz, 2 * OCTET * ssm_lam_re.shape[-1]), F32)

    def s5(zs, h0, layer, cb):
        y, hfin = _s5(_to_chunk_layout(zs), ops, h0, layer, bsz, cb)
        return _from_chunk_layout(y, bsz), hfin

    xc = ctx
    for l in range(depth):
        want_ctx = l < depth - 1
        zc_x, zs_x, q_x, k_x, v_x = _inproj(x, mod_x[l], g_pre_mix, w_a, w_bt, q_col, k_norm, tabs, l, dims, True, tm_x)
        zc_c, zs_c, q_c, k_c, v_c = _inproj(xc, mod_c[l], g_pre_mix, w_a, w_bt, q_col, k_norm, tabs, l, dims, False, tm_c)
        y_c, h_ctx = s5(zs_c, h_zero, l, cb_c)
        y_x, _ = s5(zs_x, h_ctx, l, cb_x)
        at_x = _attention(q_x, [(k_x, v_x), (k_c, v_c)], hd, tq_x)
        x = _outproj(zc_x, y_x, zs_x, at_x, x, mod_x[l], conv_w, ssm_d, w_glu, b_glu, w_out,
                     g_post_mix, l, dims, tm_x)
        x = _ffn(x, mod_x[l], g_pre_ffn, g_post_ffn, w_gate, w_up, w_down, l, tm_x, th)
        if want_ctx:
            at_c = _attention(q_c, [(k_c, v_c)], hd, tq_c)
            xc = _outproj(zc_c, y_c, zs_c, at_c, xc, mod_c[l], conv_w, ssm_d, w_glu, b_glu, w_out,
                          g_post_mix, l, dims, tm_c)
            xc = _ffn(xc, mod_c[l], g_pre_ffn, g_post_ffn, w_gate, w_up, w_down, l, tm_c, th)
    return x
```

```python
import functools
import math

import jax
import jax.numpy as jnp
from jax import lax
from jax.experimental import pallas as pl
from jax.experimental.pallas import tpu as pltpu

F32 = jnp.float32
BF16 = jnp.bfloat16

GRID_W = 64
N_KV_HEADS = 2
N_MOD = 6
ROPE_THETA = 10000.0
RMS_EPS = 1e-6

LANES = 128
SUBLANES = 8
BF16_ROWS = 16
V7X_VMEM_LIMIT_BYTES = 56 * 1024 * 1024

S5_GROUP = 16
S5_CHUNK = LANES // S5_GROUP
OCTET = LANES // S5_GROUP

MOD_ROWS = SUBLANES


def _dot(a, b):
    return jnp.dot(a, b, preferred_element_type=F32)


def _rms(x, gain):
    ms = jnp.mean(x * x, axis=-1, keepdims=True)
    return x * lax.rsqrt(ms + RMS_EPS) * gain


def _layer_spec(shape, layer):
    nd = len(shape)
    return pl.BlockSpec((1,) + shape, lambda *_: (layer,) + (0,) * nd, pipeline_mode=pl.Buffered(1))


def _params(semantics):
    return pltpu.CompilerParams(dimension_semantics=semantics, vmem_limit_bytes=V7X_VMEM_LIMIT_BYTES)


def _mod_kernel(c_ref, w_ref, b_ref, o_ref):
    s = jax.nn.silu(c_ref[...]).astype(BF16)
    o_ref[0] = _dot(s, w_ref[0].astype(BF16)) + b_ref[0]


def _modulation(cvec, w_mod, b_mod):
    depth, d, n = w_mod.shape
    r = cvec.shape[0]
    tn = min(n, 1024)
    return pl.pallas_call(
        _mod_kernel,
        grid=(depth, n // tn),
        in_specs=[
            pl.BlockSpec((r, d), lambda l, j: (0, 0)),
            pl.BlockSpec((1, d, tn), lambda l, j: (l, 0, j)),
            pl.BlockSpec((1, 1, tn), lambda l, j: (l, 0, j)),
        ],
        out_specs=pl.BlockSpec((1, r, tn), lambda l, j: (l, 0, j)),
        out_shape=jax.ShapeDtypeStruct((depth, r, n), F32),
        compiler_params=_params(("arbitrary", "arbitrary")),
        name="modulation",
    )(cvec, w_mod, b_mod.reshape(depth, 1, n))


def _wt_kernel(blk_ref, w_ref, o_ref):
    del blk_ref
    o_ref[0] = w_ref[0].T.astype(BF16)


def _transposed_columns(w, col_blocks, tn):
    depth, d, _ = w.shape
    blocks = jnp.asarray(col_blocks, jnp.int32)
    return pl.pallas_call(
        _wt_kernel,
        grid_spec=pltpu.PrefetchScalarGridSpec(
            num_scalar_prefetch=1,
            grid=(depth, len(col_blocks)),
            in_specs=[pl.BlockSpec((1, d, tn), lambda l, j, blk: (l, 0, blk[j]))],
            out_specs=pl.BlockSpec((1, tn, d), lambda l, j, blk: (l, j, 0)),
        ),
        out_shape=jax.ShapeDtypeStruct((depth, len(col_blocks) * tn, d), BF16),
        compiler_params=_params(("arbitrary", "arbitrary")),
        name="transpose_weights",
    )(blocks, w)


def _swap32(y, lane_lo):
    return jnp.where(lane_lo, pltpu.roll(y, LANES - 32, 1), pltpu.roll(y, 32, 1))


def _swap32_rows(y):
    return jnp.concatenate([y[32:64], y[0:32], y[96:128], y[64:96]], axis=0)


def _inproj_kernel(x_ref, mod_ref, g_ref, wa_ref, wbt_ref, qg_ref, kg_ref, cos_ref, sin_ref, cost_ref, sint_ref,
                   zc_ref, zs_ref, qt_ref, k_ref, vt_ref, *, cw, sw, aw, hd, rope):
    x = x_ref[0]
    h = _rms(x, g_ref[0] * (1.0 + mod_ref[0, 1:2, :])) + mod_ref[0, 0:1, :]
    h = h.astype(BF16)
    kvw = N_KV_HEADS * hd
    k_off = 3 * cw + sw
    zc_ref[0] = _dot(h, wa_ref[0, :, 0:3 * cw]).astype(BF16)
    zs_ref[0] = _dot(h, wa_ref[0, :, 3 * cw:k_off]).astype(BF16)

    tm = x.shape[0]
    nt = (((1,), (1,)), ((), ()))
    vt_ref[0] = lax.dot_general(wbt_ref[0, aw:aw + kvw, :], h, nt, preferred_element_type=F32).astype(BF16)

    lane_lo = (lax.broadcasted_iota(jnp.int32, (tm, hd), 1) & 32) == 0
    k = _dot(h, wa_ref[0, :, k_off + aw:k_off + aw + kvw])
    for i in range(N_KV_HEADS):
        y = _rms(k[:, i * hd:(i + 1) * hd], kg_ref[0])
        if rope:
            y = y * cos_ref[...] + _swap32(y, lane_lo) * sin_ref[...]
        k_ref[0, :, i * hd:(i + 1) * hd] = y.astype(BF16)

    qscale = hd ** -0.5 * math.log2(math.e)
    qt = lax.dot_general(wbt_ref[0, 0:aw, :], h, nt, preferred_element_type=F32)
    for i in range(aw // hd):
        z = qt[i * hd:(i + 1) * hd, :]
        ms = jnp.mean(z * z, axis=0, keepdims=True)
        y = z * lax.rsqrt(ms + RMS_EPS) * qg_ref[0]
        if rope:
            y = y * cost_ref[...] + _swap32_rows(y) * sint_ref[...]
        qt_ref[0, i * hd:(i + 1) * hd, :] = (y * qscale).astype(BF16)


def _inproj(x, mod, gain, w_a, w_bt, q_gain_col, k_gain, tabs, layer, dims, rope, tm):
    b, t, d = x.shape
    cw, sw, aw, hd = dims
    kvw = N_KV_HEADS * hd
    cos, sin, cos_t, sin_t = tabs
    tok = lambda w: pl.BlockSpec((1, tm, w), lambda bi, i: (bi, i, 0))
    tok_t = lambda w: pl.BlockSpec((1, w, tm), lambda bi, i: (bi, 0, i))
    tab = pl.BlockSpec((tm, hd), lambda bi, i: (i, 0))
    tab_t = pl.BlockSpec((hd, tm), lambda bi, i: (0, i))
    sds = jax.ShapeDtypeStruct
    return pl.pallas_call(
        functools.partial(_inproj_kernel, cw=cw, sw=sw, aw=aw, hd=hd, rope=rope),
        grid=(b, t // tm),
        in_specs=[
            tok(d),
            pl.BlockSpec((1, MOD_ROWS, d), lambda bi, i: (bi, 0, 0)),
            _layer_spec((1, d), layer),
            _layer_spec((d, w_a.shape[-1]), layer),
            _layer_spec((aw + kvw, d), layer),
            _layer_spec((hd, 1), layer),
            _layer_spec((1, hd), layer),
            tab, tab, tab_t, tab_t,
        ],
        out_specs=[tok(3 * cw), tok(sw), tok_t(aw), tok(kvw), tok_t(kvw)],
        out_shape=[sds((b, t, 3 * cw), BF16), sds((b, t, sw), BF16), sds((b, aw, t), BF16),
                   sds((b, t, kvw), BF16), sds((b, kvw, t), BF16)],
        compiler_params=_params(("arbitrary", "arbitrary")),
        name="inproj",
    )(x, mod, gain, w_a, w_bt, q_gain_col, k_gain, cos, sin, cos_t, sin_t)


def _s5_kernel(x_ref, atc_ref, bfc_ref, bbc_ref, cfc_ref, cbc_ref, ra_ref, rb_ref, lam_ref, h0_ref,
               y_ref, hfin_ref,
               at_ref, bf_ref, bb_ref, cf_ref, cb_ref, s_ref, hb_all_ref, h_ref,
               *, cb, nb, nblk, n_shift, p_shift):
    s = pl.program_id(1)
    half = h_ref.shape[-1] // 2
    blk = cb * nb

    def expand(comp_ref, r_ref, row_shift, col_shift):
        full = _dot(comp_ref[0, 0], r_ref[...])
        rg = (lax.broadcasted_iota(jnp.int32, full.shape, 0) >> row_shift) & (OCTET - 1)
        cg = (lax.broadcasted_iota(jnp.int32, full.shape, 1) >> col_shift) & (OCTET - 1)
        return jnp.where(rg == cg, full, 0.0).astype(at_ref.dtype)

    @pl.when(s == 0)
    def _():
        at_ref[...] = expand(atc_ref, ra_ref, n_shift, n_shift)
        bf_ref[...] = expand(bfc_ref, rb_ref, n_shift, p_shift)
        bb_ref[...] = expand(bbc_ref, rb_ref, n_shift, p_shift)
        cf_ref[...] = expand(cfc_ref, ra_ref, p_shift, n_shift)
        cb_ref[...] = expand(cbc_ref, ra_ref, p_shift, n_shift)
        h_ref[...] = h0_ref[1, 0]

    def scan(lam, reverse):
        lr = jnp.broadcast_to(lam[:, :half], (nb, half))
        li = jnp.broadcast_to(lam[:, half:], (nb, half))

        def body(k, carry):
            hr, hi = carry
            c = (cb - 1 - k) if reverse else k
            r0 = pl.multiple_of(c * nb, nb)
            srow = s_ref[pl.ds(r0, nb), :]
            s_ref[pl.ds(r0, nb), :] = jnp.concatenate([hr, hi], axis=1)
            nhr = lr * hr - li * hi + srow[:, :half]
            nhi = lr * hi + li * hr + srow[:, half:]
            return nhr, nhi

        hr, hi = lax.fori_loop(0, cb, body, (h_ref[:, :half], h_ref[:, half:]), unroll=2)
        h_ref[...] = jnp.concatenate([hr, hi], axis=1)

    @pl.when(s < nblk)
    def _():
        s_ref[...] = _dot(x_ref[0], bb_ref[...])
        scan(lam_ref[1, 0, 0], True)
        r0 = pl.multiple_of((nblk - 1 - s) * blk, blk)
        hb_all_ref[pl.ds(r0, blk), :] = s_ref[...].astype(hb_all_ref.dtype)

    @pl.when(s == nblk - 1)
    def _():
        hfin_ref[1, 0] = h_ref[...]
        h_ref[...] = h0_ref[0, 0]

    @pl.when(s >= nblk)
    def _():
        x = x_ref[0]
        s_ref[...] = _dot(x, bf_ref[...])
        scan(lam_ref[0, 0, 0], False)
        r0 = pl.multiple_of((s - nblk) * blk, blk)
        y = (_dot(x, at_ref[...]) + _dot(s_ref[...].astype(cf_ref.dtype), cf_ref[...])
             + _dot(hb_all_ref[pl.ds(r0, blk), :], cb_ref[...]))
        y_ref[0] = y.astype(y_ref.dtype)

    @pl.when(s == 2 * nblk - 1)
    def _():
        hfin_ref[0, 0] = h_ref[...]


def _s5(xq, ops, h0, layer, nb, cb):
    at_c, bf_c, bb_c, cf_c, cb_c, r_a, r_b, lam8, n_shift, p_shift = ops
    n_oct, rows, kx = xq.shape
    sw = lam8.shape[-1]
    blk = cb * nb
    nblk = rows // blk
    xblk = lambda o, s: (o, jnp.where(s < nblk, nblk - 1 - s, s - nblk), 0)
    yblk = lambda o, s: (o, jnp.maximum(s - nblk, 0), 0)
    comp = lambda a: pl.BlockSpec((1, 1) + a.shape[2:], lambda o, s: (layer, o, 0, 0))
    whole = lambda a: pl.BlockSpec(a.shape, lambda o, s: (0, 0))
    dt = xq.dtype
    return pl.pallas_call(
        functools.partial(_s5_kernel, cb=cb, nb=nb, nblk=nblk, n_shift=n_shift, p_shift=p_shift),
        grid=(n_oct, 2 * nblk),
        in_specs=[
            pl.BlockSpec((1, blk, kx), xblk),
            comp(at_c), comp(bf_c), comp(bb_c), comp(cf_c), comp(cb_c), whole(r_a), whole(r_b),
            pl.BlockSpec((2, 1, 1, 1, sw), lambda o, s: (0, layer, o, 0, 0)),
            pl.BlockSpec((2, 1, nb, sw), lambda o, s: (0, o, 0, 0)),
        ],
        out_specs=[
            pl.BlockSpec((1, blk, kx), yblk),
            pl.BlockSpec((2, 1, nb, sw), lambda o, s: (0, o, 0, 0)),
        ],
        out_shape=[
            jax.ShapeDtypeStruct((n_oct, rows, kx), dt),
            jax.ShapeDtypeStruct((2, n_oct, nb, sw), F32),
        ],
        scratch_shapes=[
            pltpu.VMEM((kx, kx), dt), pltpu.VMEM((kx, sw), dt), pltpu.VMEM((kx, sw), dt),
            pltpu.VMEM((sw, kx), dt), pltpu.VMEM((sw, kx), dt),
            pltpu.VMEM((blk, sw), F32), pltpu.VMEM((rows, sw), dt), pltpu.VMEM((nb, sw), F32),
        ],
        compiler_params=_params(("arbitrary", "arbitrary")),
        name="s5_mixer",
    )(xq, at_c, bf_c, bb_c, cf_c, cb_c, r_a, r_b, lam8, h0)


def _cmul(a, b):
    return a[0] * b[0] - a[1] * b[1], a[0] * b[1] + a[1] * b[0]


def _s5_operators(lam_re, lam_im, log_dt, b_re, b_im, c_re, c_im):
    hp = lax.Precision.HIGHEST
    depth, _, g, p = lam_re.shape
    n = b_re.shape[-1]
    n_oct = g // OCTET
    m = S5_CHUNK
    assert n == S5_GROUP and p & (p - 1) == 0 and 2 * p == LANES
    dt = jnp.exp(log_dt)[..., None]
    mag = jnp.exp(lam_re * dt)
    lam_bar = (mag * jnp.cos(lam_im * dt), mag * jnp.sin(lam_im * dt))
    den = lam_re * lam_re + lam_im * lam_im
    num = (lam_bar[0] - 1.0, lam_bar[1])
    ratio = ((num[0] * lam_re + num[1] * lam_im) / den, (num[1] * lam_re - num[0] * lam_im) / den)
    bbar = _cmul((ratio[0][..., None], ratio[1][..., None]), (b_re, b_im))
    pows = [(jnp.ones_like(lam_re), jnp.zeros_like(lam_re))]
    for _ in range(m):
        pows.append(_cmul(pows[-1], lam_bar))
    pw = (jnp.stack([q[0] for q in pows]), jnp.stack([q[1] for q in pows]))

    def rows_by_token(blocks):
        w = blocks[0].shape[-1]
        st = jnp.stack([b.reshape(depth, n_oct, OCTET, n, w) for b in blocks], axis=2)
        return st.reshape(depth, n_oct, m * OCTET * n, w)

    def cols_by_token(blocks):
        st = jnp.stack(blocks, axis=-2)
        return st.reshape(st.shape[:-2] + (m * n,))

    def compact_in(e, d, exps):
        return rows_by_token([jnp.swapaxes(e[k][:, d], -1, -2) for k in exps])

    def compact_out(f, d, exps):
        return cols_by_token([jnp.swapaxes(f[k][:, d], -1, -2) for k in exps]).reshape(
            depth, n_oct, OCTET * p, m * n)

    e = [_cmul((pw[0][k][..., None], pw[1][k][..., None]), bbar) for k in range(m + 1)]
    f = [_cmul((c_re, c_im), (pw[0][k][:, :, :, None, :], pw[1][k][:, :, :, None, :])) for k in range(m + 1)]
    er, ei = [q[0] for q in e], [q[1] for q in e]
    fr, fi = [q[0] for q in f], [-q[1] for q in f]
    fexp = [m - 1 - i for i in range(m)]
    bexp = list(range(m))
    fout = [j + 1 for j in range(m)]
    bout = [m - j for j in range(m)]
    b_f = jnp.concatenate([compact_in(er, 0, fexp), compact_in(ei, 0, fexp)], axis=-1)
    b_b = jnp.concatenate([compact_in(er, 1, bexp), compact_in(ei, 1, bexp)], axis=-1)
    c_f = jnp.concatenate([compact_out(fr, 0, fout), compact_out(fi, 0, fout)], axis=-2)
    c_b = jnp.concatenate([compact_out(fr, 1, bout), compact_out(fi, 1, bout)], axis=-2)

    kk = jnp.stack([
        jnp.einsum('ldgop,ldgpn->ldgon', f[k][0], bbar[0], precision=hp)
        - jnp.einsum('ldgop,ldgpn->ldgon', f[k][1], bbar[1], precision=hp)
        for k in range(m)])
    kt = jnp.swapaxes(kk, -1, -2)

    def toeplitz_block(i, j):
        if j > i:
            return kt[j - i, :, 0]
        if j < i:
            return kt[i - j, :, 1]
        return kt[0, :, 0] + kt[0, :, 1]

    a_tot = rows_by_token([cols_by_token([toeplitz_block(i, j) for j in range(m)]) for i in range(m)])

    col = jnp.arange(OCTET * LANES)
    row = jnp.arange(LANES)
    r_a = (row[:, None] // n == col[None, :] // (OCTET * n)) & (row[:, None] % n == col[None, :] % n)
    r_b = (row[:, None] // p == col[None, :] // (OCTET * p)) & (row[:, None] % p == col[None, :] % p)

    lam8 = jnp.concatenate([pw[0][m].reshape(depth, 2, n_oct, 1, OCTET * p),
                            pw[1][m].reshape(depth, 2, n_oct, 1, OCTET * p)], axis=-1)
    lam8 = jnp.moveaxis(lam8, 1, 0)
    cast = lambda a: a.astype(BF16)
    return (cast(a_tot), cast(b_f), cast(b_b), cast(c_f), cast(c_b), cast(r_a), cast(r_b), lam8,
            n.bit_length() - 1, p.bit_length() - 1)


def _to_chunk_layout(u):
    b, t, w = u.shape
    n_oct = w // LANES
    u = u.reshape(b, t // S5_CHUNK, S5_CHUNK, n_oct, LANES)
    u = jnp.transpose(u, (3, 1, 0, 2, 4))
    return u.reshape(n_oct, (t // S5_CHUNK) * b, S5_CHUNK * LANES)


def _from_chunk_layout(y, b):
    n_oct, rows, _ = y.shape
    mc = rows // b
    y = y.reshape(n_oct, mc, b, S5_CHUNK, LANES)
    y = jnp.transpose(y, (2, 1, 3, 0, 4))
    return y.reshape(b, mc * S5_CHUNK, n_oct * LANES)


def _attn_kernel(*refs, group, hd, chunks):
    qt_ref = refs[0]
    ot_ref = refs[-1]
    kv = refs[1:-1]
    tq = qt_ref.shape[2]
    qt = jnp.concatenate([qt_ref[0, h * hd:(h + 1) * hd, :] for h in range(group)], axis=1)
    cols = group * tq
    m = jnp.full((1, cols), -jnp.inf, F32)
    acc = jnp.zeros((hd + BF16_ROWS, cols), F32)
    for i, tk in enumerate(chunks):
        k_ref, vt_ref = kv[2 * i], kv[2 * i + 1]
        ones = jnp.ones((BF16_ROWS, tk), BF16)
        for c in range(k_ref.shape[1] // tk):
            k = k_ref[0, c * tk:(c + 1) * tk, :]
            vt = jnp.concatenate([vt_ref[0, :, c * tk:(c + 1) * tk], ones], axis=0)
            st = _dot(k, qt)
            m_new = jnp.maximum(m, jnp.max(st, axis=0, keepdims=True))
            alpha = jnp.exp2(m - m_new)
            p = jnp.exp2(st - m_new)
            acc = alpha * acc + _dot(vt, p.astype(BF16))
            m = m_new
    ot = acc[:hd] / acc[hd:hd + 1]
    for h in range(group):
        ot_ref[0, h * hd:(h + 1) * hd, :] = ot[:, h * tq:(h + 1) * tq].astype(BF16)


def _attention(qt, kvs, hd, tq):
    b, aw, t = qt.shape
    group = aw // hd // N_KV_HEADS
    gw = group * hd
    chunks = tuple(min(k.shape[1], 512) for k, _ in kvs)
    in_specs = [pl.BlockSpec((1, gw, tq), lambda bi, j, i: (bi, j, i))]
    args = [qt]
    for k, vt in kvs:
        s = k.shape[1]
        in_specs.append(pl.BlockSpec((1, s, hd), lambda bi, j, i: (bi, 0, j)))
        in_specs.append(pl.BlockSpec((1, hd, s), lambda bi, j, i: (bi, j, 0)))
        args += [k, vt]
    return pl.pallas_call(
        functools.partial(_attn_kernel, group=group, hd=hd, chunks=chunks),
        grid=(b, N_KV_HEADS, t // tq),
        in_specs=in_specs,
        out_specs=pl.BlockSpec((1, gw, tq), lambda bi, j, i: (bi, j, i)),
        out_shape=jax.ShapeDtypeStruct((b, aw, t), BF16),
        compiler_params=_params(("arbitrary", "arbitrary", "arbitrary")),
        name="attention",
    )(*args)


def _outproj_kernel(zc_ref, zp_ref, zn_ref, y_ref, u_ref, at_ref, x_ref, mod_ref,
                    cwt_ref, d_ref, wglu_ref, bglu_ref, wo_ref, g_ref, o_ref, ubuf_ref, *, cw):
    i = pl.program_id(1)
    tm = x_ref.shape[1]

    def gated(z):
        z = z.astype(F32)
        return z[:, 2 * cw:] * z[:, :cw]

    zc = zc_ref[0].astype(F32)
    u = zc[:, 2 * cw:] * zc[:, :cw]
    prev = gated(zp_ref[0])[BF16_ROWS - 1:BF16_ROWS]
    nxt = gated(zn_ref[0])[0:1]
    ubuf_ref[SUBLANES - 1:SUBLANES, :] = jnp.where(i > 0, prev, 0.0)
    ubuf_ref[SUBLANES:SUBLANES + tm, :] = u
    ubuf_ref[SUBLANES + tm:SUBLANES + tm + 1, :] = jnp.where(i < pl.num_programs(1) - 1, nxt, 0.0)
    y = (cwt_ref[0, 0:1] * ubuf_ref[SUBLANES - 1:SUBLANES - 1 + tm, :] + cwt_ref[0, 1:2] * u
         + cwt_ref[0, 2:3] * ubuf_ref[SUBLANES + 1:SUBLANES + 1 + tm, :])
    conv = (zc[:, cw:2 * cw] * y).astype(BF16)

    ys = y_ref[0].astype(F32) + d_ref[0] * u_ref[0].astype(F32)
    gl = jax.nn.gelu(ys)
    gate = jax.nn.sigmoid(_dot(gl.astype(BF16), wglu_ref[0]) + bglu_ref[0])
    ssm = (gl * gate).astype(BF16)

    ns = conv.shape[1] + ssm.shape[1]
    tn = (((0,), (0,)), ((), ()))
    mix = (_dot(jnp.concatenate([conv, ssm], axis=1), wo_ref[0, 0:ns, :])
           + lax.dot_general(at_ref[0], wo_ref[0, ns:, :], tn, preferred_element_type=F32))
    o_ref[0] = x_ref[0] + mod_ref[0, 2:3, :] * _rms(mix, g_ref[0])


def _outproj(zc, y, u, at, x, mod, conv_w, ssm_d, w_glu, b_glu, w_out, gain, layer, dims, tm):
    b, t, d = x.shape
    cw, sw, aw, hd = dims
    nhalo = tm // BF16_ROWS
    last = t // BF16_ROWS - 1
    tok = lambda w: pl.BlockSpec((1, tm, w), lambda bi, i: (bi, i, 0))
    return pl.pallas_call(
        functools.partial(_outproj_kernel, cw=cw),
        grid=(b, t // tm),
        in_specs=[
            tok(3 * cw),
            pl.BlockSpec((1, BF16_ROWS, 3 * cw), lambda bi, i: (bi, jnp.maximum(i * nhalo - 1, 0), 0)),
            pl.BlockSpec((1, BF16_ROWS, 3 * cw), lambda bi, i: (bi, jnp.minimum((i + 1) * nhalo, last), 0)),
            tok(sw), tok(sw),
            pl.BlockSpec((1, aw, tm), lambda bi, i: (bi, 0, i)),
            tok(d),
            pl.BlockSpec((1, MOD_ROWS, d), lambda bi, i: (bi, 0, 0)),
            _layer_spec((conv_w.shape[1], cw), layer),
            _layer_spec((1, sw), layer),
            _layer_spec((sw, sw), layer),
            _layer_spec((1, sw), layer),
            _layer_spec((d, d), layer),
            _layer_spec((1, d), layer),
        ],
        out_specs=tok(d),
        out_shape=jax.ShapeDtypeStruct((b, t, d), F32),
        scratch_shapes=[pltpu.VMEM((tm + 2 * SUBLANES, cw), F32)],
        compiler_params=_params(("arbitrary", "arbitrary")),
        name="outproj",
    )(zc, zc, zc, y, u, at, x, mod, conv_w, ssm_d, w_glu, b_glu, w_out, gain)


def _ffn_kernel(x_ref, mod_ref, gpre_ref, gpost_ref, wg_ref, wu_ref, wd_ref, o_ref, h_ref, acc_ref, *, nj):
    j = pl.program_id(2)

    def hidden_tile(h):
        a = jax.nn.silu(_dot(h, wg_ref[0])) * _dot(h, wu_ref[0])
        return _dot(a.astype(BF16), wd_ref[0])

    def pre_norm():
        gain = gpre_ref[0] * (1.0 + mod_ref[0, 4:5, :])
        h = (_rms(x_ref[0], gain) + mod_ref[0, 3:4, :]).astype(BF16)
        h_ref[...] = h
        return h

    def post_norm(acc):
        o_ref[0] = x_ref[0] + mod_ref[0, 5:6, :] * _rms(acc, gpost_ref[0])

    if nj == 1:
        post_norm(hidden_tile(pre_norm()))
        return

    @pl.when(j == 0)
    def _():
        acc_ref[...] = hidden_tile(pre_norm())

    @pl.when((j > 0) & (j < nj - 1))
    def _():
        acc_ref[...] += hidden_tile(h_ref[...])

    @pl.when(j == nj - 1)
    def _():
        post_norm(acc_ref[...] + hidden_tile(h_ref[...]))


def _ffn(x, mod, g_pre, g_post, w_gate, w_up, w_down, layer, tm, th):
    b, t, d = x.shape
    f = w_gate.shape[-1]
    tok = pl.BlockSpec((1, tm, d), lambda bi, i, j: (bi, i, 0))
    return pl.pallas_call(
        functools.partial(_ffn_kernel, nj=f // th),
        grid=(b, t // tm, f // th),
        in_specs=[
            tok,
            pl.BlockSpec((1, MOD_ROWS, d), lambda bi, i, j: (bi, 0, 0)),
            _layer_spec((1, d), layer),
            _layer_spec((1, d), layer),
            pl.BlockSpec((1, d, th), lambda bi, i, j: (layer, 0, j)),
            pl.BlockSpec((1, d, th), lambda bi, i, j: (layer, 0, j)),
            pl.BlockSpec((1, th, d), lambda bi, i, j: (layer, j, 0)),
        ],
        out_specs=tok,
        out_shape=jax.ShapeDtypeStruct((b, t, d), F32),
        scratch_shapes=[pltpu.VMEM((tm, d), BF16), pltpu.VMEM((tm, d), F32)],
        input_output_aliases={0: 0},
        compiler_params=_params(("arbitrary", "arbitrary", "arbitrary")),
        name="ffn",
    )(x, mod, g_pre, g_post, w_gate, w_up, w_down)


def _rope_tables(t, hd):
    rows = t // GRID_W
    row = jnp.broadcast_to(jnp.arange(rows)[:, None], (rows, GRID_W)).reshape(-1).astype(F32)
    col = jnp.broadcast_to(jnp.arange(GRID_W)[None, :], (rows, GRID_W)).reshape(-1).astype(F32)
    half = hd // 2
    inv_freq = ROPE_THETA ** (-jnp.arange(0, half, 2, dtype=F32) / half)
    ar = row[:, None] * inv_freq
    ac = col[:, None] * inv_freq
    cos = jnp.concatenate([jnp.cos(ar), jnp.cos(ar), jnp.cos(ac), jnp.cos(ac)], axis=-1)
    sin = jnp.concatenate([-jnp.sin(ar), jnp.sin(ar), -jnp.sin(ac), jnp.sin(ac)], axis=-1)
    return cos, sin


def _tile(n, pref):
    return pref if n % pref == 0 else n


def kernel(x, c, ctx, c_ctx, w_mod, b_mod, g_pre_mix, g_post_mix, g_pre_ffn, g_post_ffn, w_in, conv_w, ssm_lam_re, ssm_lam_im, ssm_log_dt, ssm_b_re, ssm_b_im, ssm_c_re, ssm_c_im, ssm_d, w_glu, b_glu, q_norm, k_norm, w_out, w_gate, w_up, w_down):
    bsz, t, d = x.shape
    n_ctx = ctx.shape[1]
    depth = w_mod.shape[0]
    cw = conv_w.shape[-1]
    sw = w_glu.shape[-1]
    hd = q_norm.shape[-1]
    aw = d - cw - sw
    dims = (cw, sw, aw, hd)
    assert w_in.shape[-1] == 3 * cw + sw + aw + 2 * N_KV_HEADS * hd
    assert sw % LANES == 0 and t % (S5_CHUNK * BF16_ROWS) == 0 and n_ctx % (S5_CHUNK * BF16_ROWS) == 0

    cvec = jnp.concatenate([c, c_ctx[None, :], jnp.zeros((BF16_ROWS - bsz - 1, d), F32)], axis=0)
    mod = _modulation(cvec, w_mod, b_mod).reshape(depth, BF16_ROWS, N_MOD, d)
    mod = jnp.pad(mod, ((0, 0), (0, 0), (0, MOD_ROWS - N_MOD), (0, 0)))
    mod_x = mod[:, :bsz]
    mod_c = jnp.broadcast_to(mod[:, bsz:bsz + 1], (depth, bsz, MOD_ROWS, d))

    cos, sin = _rope_tables(t, hd)
    tabs = (cos, sin, cos.T, sin.T)
    ops = _s5_operators(ssm_lam_re, ssm_lam_im, ssm_log_dt, ssm_b_re, ssm_b_im, ssm_c_re, ssm_c_im)
    n_oct = sw // LANES

    r3 = lambda a: a.reshape(depth, 1, a.shape[-1])
    g_pre_mix, g_post_mix, g_pre_ffn, g_post_ffn = map(r3, (g_pre_mix, g_post_mix, g_pre_ffn, g_post_ffn))
    q_col = q_norm.reshape(depth, hd, 1)
    k_norm, ssm_d, b_glu = map(r3, (k_norm, ssm_d, b_glu))
    q_off = 3 * cw + sw
    k_off = q_off + aw
    v_off = k_off + N_KV_HEADS * hd
    w_a = w_in.astype(BF16)
    assert q_off % LANES == 0 and v_off % LANES == 0 and aw % LANES == 0
    qv_blocks = ([q_off // LANES + i for i in range(aw // LANES)]
                 + [v_off // LANES + i for i in range(N_KV_HEADS * hd // LANES)])
    w_bt = _transposed_columns(w_in, qv_blocks, LANES)
    w_glu, w_out, w_gate, w_up, w_down = (w.astype(BF16) for w in (w_glu, w_out, w_gate, w_up, w_down))

    tm_x, tm_c = _tile(t, 512), _tile(n_ctx, 256)
    tq_x, tq_c = _tile(t, 1024), _tile(n_ctx, 256)
    f = w_gate.shape[-1]
    th = max(c for c in range(LANES, 512 + 1, LANES) if f % c == 0)
    cb_x = _tile(t // S5_CHUNK, 128)
    cb_c = n_ctx // S5_CHUNK
    h_zero = jnp.zeros((2, n_oct, bsz, 2 * OCTET * ssm_lam_re.shape[-1]), F32)

    def s5(zs, h0, layer, cb):
        y, hfin = _s5(_to_chunk_layout(zs), ops, h0, layer, bsz, cb)
        return _from_chunk_layout(y, bsz), hfin

    xc = ctx
    for l in range(depth):
        want_ctx = l < depth - 1
        zc_x, zs_x, q_x, k_x, v_x = _inproj(x, mod_x[l], g_pre_mix, w_a, w_bt, q_col, k_norm, tabs, l, dims, True, tm_x)
        zc_c, zs_c, q_c, k_c, v_c = _inproj(xc, mod_c[l], g_pre_mix, w_a, w_bt, q_col, k_norm, tabs, l, dims, False, tm_c)
        y_c, h_ctx = s5(zs_c, h_zero, l, cb_c)
        y_x, _ = s5(zs_x, h_ctx, l, cb_x)
        at_x = _attention(q_x, [(k_x, v_x), (k_c, v_c)], hd, tq_x)
        x = _outproj(zc_x, y_x, zs_x, at_x, x, mod_x[l], conv_w, ssm_d, w_glu, b_glu, w_out,
                     g_post_mix, l, dims, tm_x)
        x = _ffn(x, mod_x[l], g_pre_ffn, g_post_ffn, w_gate, w_up, w_down, l, tm_x, th)
        if want_ctx:
            at_c = _attention(q_c, [(k_c, v_c)], hd, tq_c)
            xc = _outproj(zc_c, y_c, zs_c, at_c, xc, mod_c[l], conv_w, ssm_d, w_glu, b_glu, w_out,
                          g_post_mix, l, dims, tm_c)
            xc = _ffn(xc, mod_c[l], g_pre_ffn, g_post_ffn, w_gate, w_up, w_down, l, tm_c, th)
    return x
```

```python
import functools
import math

import jax
import jax.numpy as jnp
from jax import lax
from jax.experimental import pallas as pl
from jax.experimental.pallas import tpu as pltpu

F32 = jnp.float32
BF16 = jnp.bfloat16

GRID_W = 64
N_KV_HEADS = 2
N_MOD = 6
ROPE_THETA = 10000.0
RMS_EPS = 1e-6

LANES = 128
SUBLANES = 8
BF16_ROWS = 16
V7X_VMEM_LIMIT_BYTES = 56 * 1024 * 1024

S5_GROUP = 16
S5_CHUNK = LANES // S5_GROUP
OCTET = LANES // S5_GROUP

MOD_ROWS = SUBLANES


def _dot(a, b):
    return jnp.dot(a, b, preferred_element_type=F32)


def _rms(x, gain):
    ms = jnp.mean(x * x, axis=-1, keepdims=True)
    return x * lax.rsqrt(ms + RMS_EPS) * gain


def _layer_spec(shape, layer):
    nd = len(shape)
    return pl.BlockSpec((1,) + shape, lambda *_: (layer,) + (0,) * nd, pipeline_mode=pl.Buffered(1))


def _params(semantics):
    return pltpu.CompilerParams(dimension_semantics=semantics, vmem_limit_bytes=V7X_VMEM_LIMIT_BYTES)


def _mod_kernel(c_ref, w_ref, b_ref, o_ref):
    s = jax.nn.silu(c_ref[...]).astype(BF16)
    o_ref[0] = _dot(s, w_ref[0].astype(BF16)) + b_ref[0]


def _modulation(cvec, w_mod, b_mod):
    depth, d, n = w_mod.shape
    r = cvec.shape[0]
    tn = min(n, 1024)
    return pl.pallas_call(
        _mod_kernel,
        grid=(depth, n // tn),
        in_specs=[
            pl.BlockSpec((r, d), lambda l, j: (0, 0)),
            pl.BlockSpec((1, d, tn), lambda l, j: (l, 0, j)),
            pl.BlockSpec((1, 1, tn), lambda l, j: (l, 0, j)),
        ],
        out_specs=pl.BlockSpec((1, r, tn), lambda l, j: (l, 0, j)),
        out_shape=jax.ShapeDtypeStruct((depth, r, n), F32),
        compiler_params=_params(("arbitrary", "arbitrary")),
        name="modulation",
    )(cvec, w_mod, b_mod.reshape(depth, 1, n))


def _wt_kernel(blk_ref, w_ref, o_ref):
    del blk_ref
    o_ref[0] = w_ref[0].T.astype(BF16)


def _transposed_columns(w, col_blocks, tn):
    depth, d, _ = w.shape
    blocks = jnp.asarray(col_blocks, jnp.int32)
    return pl.pallas_call(
        _wt_kernel,
        grid_spec=pltpu.PrefetchScalarGridSpec(
            num_scalar_prefetch=1,
            grid=(depth, len(col_blocks)),
            in_specs=[pl.BlockSpec((1, d, tn), lambda l, j, blk: (l, 0, blk[j]))],
            out_specs=pl.BlockSpec((1, tn, d), lambda l, j, blk: (l, j, 0)),
        ),
        out_shape=jax.ShapeDtypeStruct((depth, len(col_blocks) * tn, d), BF16),
        compiler_params=_params(("arbitrary", "arbitrary")),
        name="transpose_weights",
    )(blocks, w)


def _swap32(y, lane_lo):
    return jnp.where(lane_lo, pltpu.roll(y, LANES - 32, 1), pltpu.roll(y, 32, 1))


def _swap32_rows(y):
    return jnp.concatenate([y[32:64], y[0:32], y[96:128], y[64:96]], axis=0)


def _inproj_kernel(x_ref, mod_ref, g_ref, wa_ref, wbt_ref, qg_ref, kg_ref, cos_ref, sin_ref, cost_ref, sint_ref,
                   zc_ref, zq_ref, qt_ref, k_ref, vt_ref, zs_scr, *, cw, sw, aw, hd, rope):
    x = x_ref[0]
    h = _rms(x, g_ref[0] * (1.0 + mod_ref[0, 1:2, :])) + mod_ref[0, 0:1, :]
    h = h.astype(BF16)
    kvw = N_KV_HEADS * hd
    k_off = 3 * cw + sw
    zc_ref[0] = _dot(h, wa_ref[0, :, 0:3 * cw]).astype(BF16)
    zs = _dot(h, wa_ref[0, :, 3 * cw:k_off])
    chunks = x.shape[0] // S5_CHUNK
    for o in range(sw // LANES):
        zs_scr[o] = zs[:, o * LANES:(o + 1) * LANES]
        for i in range(S5_CHUNK):
            piece = zs_scr[o, pl.ds(i, chunks, stride=S5_CHUNK), :]
            zq_ref[o, 0, :, i * LANES:(i + 1) * LANES] = piece.astype(BF16)

    tm = x.shape[0]
    nt = (((1,), (1,)), ((), ()))
    vt_ref[0] = lax.dot_general(wbt_ref[0, aw:aw + kvw, :], h, nt, preferred_element_type=F32).astype(BF16)

    lane_lo = (lax.broadcasted_iota(jnp.int32, (tm, hd), 1) & 32) == 0
    k = _dot(h, wa_ref[0, :, k_off + aw:k_off + aw + kvw])
    for i in range(N_KV_HEADS):
        y = _rms(k[:, i * hd:(i + 1) * hd], kg_ref[0])
        if rope:
            y = y * cos_ref[...] + _swap32(y, lane_lo) * sin_ref[...]
        k_ref[0, :, i * hd:(i + 1) * hd] = y.astype(BF16)

    qscale = hd ** -0.5 * math.log2(math.e)
    qt = lax.dot_general(wbt_ref[0, 0:aw, :], h, nt, preferred_element_type=F32)
    for i in range(aw // hd):
        z = qt[i * hd:(i + 1) * hd, :]
        ms = jnp.mean(z * z, axis=0, keepdims=True)
        y = z * lax.rsqrt(ms + RMS_EPS) * qg_ref[0]
        if rope:
            y = y * cost_ref[...] + _swap32_rows(y) * sint_ref[...]
        qt_ref[0, i * hd:(i + 1) * hd, :] = (y * qscale).astype(BF16)


def _inproj(x, mod, gain, w_a, w_bt, q_gain_col, k_gain, tabs, layer, dims, rope, tm):
    b, t, d = x.shape
    cw, sw, aw, hd = dims
    kvw = N_KV_HEADS * hd
    n_oct = sw // LANES
    cos, sin, cos_t, sin_t = tabs
    tok = lambda w: pl.BlockSpec((1, tm, w), lambda bi, i: (bi, i, 0))
    tok_t = lambda w: pl.BlockSpec((1, w, tm), lambda bi, i: (bi, 0, i))
    tab = pl.BlockSpec((tm, hd), lambda bi, i: (i, 0))
    tab_t = pl.BlockSpec((hd, tm), lambda bi, i: (0, i))
    sds = jax.ShapeDtypeStruct
    return pl.pallas_call(
        functools.partial(_inproj_kernel, cw=cw, sw=sw, aw=aw, hd=hd, rope=rope),
        grid=(b, t // tm),
        in_specs=[
            tok(d),
            pl.BlockSpec((1, MOD_ROWS, d), lambda bi, i: (bi, 0, 0)),
            _layer_spec((1, d), layer),
            _layer_spec((d, w_a.shape[-1]), layer),
            _layer_spec((aw + kvw, d), layer),
            _layer_spec((hd, 1), layer),
            _layer_spec((1, hd), layer),
            tab, tab, tab_t, tab_t,
        ],
        out_specs=[tok(3 * cw),
                   pl.BlockSpec((n_oct, 1, tm // S5_CHUNK, S5_CHUNK * LANES), lambda bi, i: (0, bi, i, 0)),
                   tok_t(aw), tok(kvw), tok_t(kvw)],
        out_shape=[sds((b, t, 3 * cw), BF16), sds((n_oct, b, t // S5_CHUNK, S5_CHUNK * LANES), BF16),
                   sds((b, aw, t), BF16), sds((b, t, kvw), BF16), sds((b, kvw, t), BF16)],
        scratch_shapes=[pltpu.VMEM((n_oct, tm, LANES), F32)],
        compiler_params=_params(("arbitrary", "arbitrary")),
        name="inproj",
    )(x, mod, gain, w_a, w_bt, q_gain_col, k_gain, cos, sin, cos_t, sin_t)


def _s5_kernel(x_ref, atc_ref, bfc_ref, bbc_ref, cfc_ref, cbc_ref, ra_ref, rb_ref, lam_ref, dvec_ref, h0_ref,
               y_ref, hfin_ref,
               at_ref, bf_ref, bb_ref, cf_ref, cb_ref, s_ref, hb_all_ref, h_ref,
               *, cb, nb, nblk, n_shift, p_shift):
    s = pl.program_id(1)
    half = h_ref.shape[-1] // 2
    blk = cb * nb

    def expand(comp_ref, r_ref, row_shift, col_shift):
        full = _dot(comp_ref[0, 0], r_ref[...])
        rg = (lax.broadcasted_iota(jnp.int32, full.shape, 0) >> row_shift) & (OCTET - 1)
        cg = (lax.broadcasted_iota(jnp.int32, full.shape, 1) >> col_shift) & (OCTET - 1)
        return jnp.where(rg == cg, full, 0.0).astype(at_ref.dtype)

    @pl.when(s == 0)
    def _():
        at_ref[...] = expand(atc_ref, ra_ref, n_shift, n_shift)
        bf_ref[...] = expand(bfc_ref, rb_ref, n_shift, p_shift)
        bb_ref[...] = expand(bbc_ref, rb_ref, n_shift, p_shift)
        cf_ref[...] = expand(cfc_ref, ra_ref, p_shift, n_shift)
        cb_ref[...] = expand(cbc_ref, ra_ref, p_shift, n_shift)
        h_ref[...] = h0_ref[1, 0]

    def scan(lam, reverse):
        lr = jnp.broadcast_to(lam[:, :half], (nb, half))
        li = jnp.broadcast_to(lam[:, half:], (nb, half))

        def body(k, carry):
            hr, hi = carry
            c = (cb - 1 - k) if reverse else k
            r0 = pl.multiple_of(c * nb, nb)
            srow = s_ref[pl.ds(r0, nb), :]
            s_ref[pl.ds(r0, nb), :] = jnp.concatenate([hr, hi], axis=1)
            nhr = lr * hr - li * hi + srow[:, :half]
            nhi = lr * hi + li * hr + srow[:, half:]
            return nhr, nhi

        hr, hi = lax.fori_loop(0, cb, body, (h_ref[:, :half], h_ref[:, half:]), unroll=2)
        h_ref[...] = jnp.concatenate([hr, hi], axis=1)

    @pl.when(s < nblk)
    def _():
        s_ref[...] = _dot(x_ref[0], bb_ref[...])
        scan(lam_ref[1, 0, 0], True)
        r0 = pl.multiple_of((nblk - 1 - s) * blk, blk)
        hb_all_ref[pl.ds(r0, blk), :] = s_ref[...].astype(hb_all_ref.dtype)

    @pl.when(s == nblk - 1)
    def _():
        hfin_ref[1, 0] = h_ref[...]
        h_ref[...] = h0_ref[0, 0]

    @pl.when(s >= nblk)
    def _():
        x = x_ref[0]
        s_ref[...] = _dot(x, bf_ref[...])
        scan(lam_ref[0, 0, 0], False)
        r0 = pl.multiple_of((s - nblk) * blk, blk)
        y = (_dot(x, at_ref[...]) + _dot(s_ref[...].astype(cf_ref.dtype), cf_ref[...])
             + _dot(hb_all_ref[pl.ds(r0, blk), :], cb_ref[...]))
        y_ref[0] = (y + dvec_ref[0, 0] * x.astype(F32)).astype(y_ref.dtype)

    @pl.when(s == 2 * nblk - 1)
    def _():
        hfin_ref[0, 0] = h_ref[...]


def _s5(xq, ops, h0, layer, nb, cb):
    at_c, bf_c, bb_c, cf_c, cb_c, r_a, r_b, lam8, dvec, n_shift, p_shift = ops
    n_oct, rows, kx = xq.shape
    sw = lam8.shape[-1]
    blk = cb * nb
    nblk = rows // blk
    xblk = lambda o, s: (o, jnp.where(s < nblk, nblk - 1 - s, s - nblk), 0)
    yblk = lambda o, s: (o, jnp.maximum(s - nblk, 0), 0)
    comp = lambda a: pl.BlockSpec((1, 1) + a.shape[2:], lambda o, s: (layer, o, 0, 0))
    whole = lambda a: pl.BlockSpec(a.shape, lambda o, s: (0, 0))
    dt = xq.dtype
    return pl.pallas_call(
        functools.partial(_s5_kernel, cb=cb, nb=nb, nblk=nblk, n_shift=n_shift, p_shift=p_shift),
        grid=(n_oct, 2 * nblk),
        in_specs=[
            pl.BlockSpec((1, blk, kx), xblk),
            comp(at_c), comp(bf_c), comp(bb_c), comp(cf_c), comp(cb_c), whole(r_a), whole(r_b),
            pl.BlockSpec((2, 1, 1, 1, sw), lambda o, s: (0, layer, o, 0, 0)),
            pl.BlockSpec((1, 1, 1, kx), lambda o, s: (layer, o, 0, 0)),
            pl.BlockSpec((2, 1, nb, sw), lambda o, s: (0, o, 0, 0)),
        ],
        out_specs=[
            pl.BlockSpec((1, blk, kx), yblk),
            pl.BlockSpec((2, 1, nb, sw), lambda o, s: (0, o, 0, 0)),
        ],
        out_shape=[
            jax.ShapeDtypeStruct((n_oct, rows, kx), dt),
            jax.ShapeDtypeStruct((2, n_oct, nb, sw), F32),
        ],
        scratch_shapes=[
            pltpu.VMEM((kx, kx), dt), pltpu.VMEM((kx, sw), dt), pltpu.VMEM((kx, sw), dt),
            pltpu.VMEM((sw, kx), dt), pltpu.VMEM((sw, kx), dt),
            pltpu.VMEM((blk, sw), F32), pltpu.VMEM((rows, sw), dt), pltpu.VMEM((nb, sw), F32),
        ],
        compiler_params=_params(("arbitrary", "arbitrary")),
        name="s5_mixer",
    )(xq, at_c, bf_c, bb_c, cf_c, cb_c, r_a, r_b, lam8, dvec, h0)


def _cmul(a, b):
    return a[0] * b[0] - a[1] * b[1], a[0] * b[1] + a[1] * b[0]


def _s5_operators(lam_re, lam_im, log_dt, b_re, b_im, c_re, c_im, d_skip):
    hp = lax.Precision.HIGHEST
    depth, _, g, p = lam_re.shape
    n = b_re.shape[-1]
    n_oct = g // OCTET
    m = S5_CHUNK
    assert n == S5_GROUP and p & (p - 1) == 0 and 2 * p == LANES
    dt = jnp.exp(log_dt)[..., None]
    mag = jnp.exp(lam_re * dt)
    lam_bar = (mag * jnp.cos(lam_im * dt), mag * jnp.sin(lam_im * dt))
    den = lam_re * lam_re + lam_im * lam_im
    num = (lam_bar[0] - 1.0, lam_bar[1])
    ratio = ((num[0] * lam_re + num[1] * lam_im) / den, (num[1] * lam_re - num[0] * lam_im) / den)
    bbar = _cmul((ratio[0][..., None], ratio[1][..., None]), (b_re, b_im))
    pows = [(jnp.ones_like(lam_re), jnp.zeros_like(lam_re))]
    for _ in range(m):
        pows.append(_cmul(pows[-1], lam_bar))
    pw = (jnp.stack([q[0] for q in pows]), jnp.stack([q[1] for q in pows]))

    def rows_by_token(blocks):
        w = blocks[0].shape[-1]
        st = jnp.stack([b.reshape(depth, n_oct, OCTET, n, w) for b in blocks], axis=2)
        return st.reshape(depth, n_oct, m * OCTET * n, w)

    def cols_by_token(blocks):
        st = jnp.stack(blocks, axis=-2)
        return st.reshape(st.shape[:-2] + (m * n,))

    def compact_in(e, d, exps):
        return rows_by_token([jnp.swapaxes(e[k][:, d], -1, -2) for k in exps])

    def compact_out(f, d, exps):
        return cols_by_token([jnp.swapaxes(f[k][:, d], -1, -2) for k in exps]).reshape(
            depth, n_oct, OCTET * p, m * n)

    e = [_cmul((pw[0][k][..., None], pw[1][k][..., None]), bbar) for k in range(m + 1)]
    f = [_cmul((c_re, c_im), (pw[0][k][:, :, :, None, :], pw[1][k][:, :, :, None, :])) for k in range(m + 1)]
    er, ei = [q[0] for q in e], [q[1] for q in e]
    fr, fi = [q[0] for q in f], [-q[1] for q in f]
    fexp = [m - 1 - i for i in range(m)]
    bexp = list(range(m))
    fout = [j + 1 for j in range(m)]
    bout = [m - j for j in range(m)]
    b_f = jnp.concatenate([compact_in(er, 0, fexp), compact_in(ei, 0, fexp)], axis=-1)
    b_b = jnp.concatenate([compact_in(er, 1, bexp), compact_in(ei, 1, bexp)], axis=-1)
    c_f = jnp.concatenate([compact_out(fr, 0, fout), compact_out(fi, 0, fout)], axis=-2)
    c_b = jnp.concatenate([compact_out(fr, 1, bout), compact_out(fi, 1, bout)], axis=-2)

    kk = jnp.stack([
        jnp.einsum('ldgop,ldgpn->ldgon', f[k][0], bbar[0], precision=hp)
        - jnp.einsum('ldgop,ldgpn->ldgon', f[k][1], bbar[1], precision=hp)
        for k in range(m)])
    kt = jnp.swapaxes(kk, -1, -2)

    def toeplitz_block(i, j):
        if j > i:
            return kt[j - i, :, 0]
        if j < i:
            return kt[i - j, :, 1]
        return kt[0, :, 0] + kt[0, :, 1]

    a_tot = rows_by_token([cols_by_token([toeplitz_block(i, j) for j in range(m)]) for i in range(m)])

    col = jnp.arange(OCTET * LANES)
    row = jnp.arange(LANES)
    r_a = (row[:, None] // n == col[None, :] // (OCTET * n)) & (row[:, None] % n == col[None, :] % n)
    r_b = (row[:, None] // p == col[None, :] // (OCTET * p)) & (row[:, None] % p == col[None, :] % p)

    lam8 = jnp.concatenate([pw[0][m].reshape(depth, 2, n_oct, 1, OCTET * p),
                            pw[1][m].reshape(depth, 2, n_oct, 1, OCTET * p)], axis=-1)
    lam8 = jnp.moveaxis(lam8, 1, 0)
    dvec = jnp.tile(d_skip.reshape(depth, n_oct, 1, LANES), (1, 1, 1, m))
    cast = lambda a: a.astype(BF16)
    return (cast(a_tot), cast(b_f), cast(b_b), cast(c_f), cast(c_b), cast(r_a), cast(r_b), lam8, dvec,
            n.bit_length() - 1, p.bit_length() - 1)


def _batch_minor_rows(zq):
    n_oct, b, chunks, w = zq.shape
    return jnp.swapaxes(zq, 1, 2).reshape(n_oct, chunks * b, w)


def _from_chunk_layout(y, b):
    n_oct, rows, _ = y.shape
    mc = rows // b
    y = y.reshape(n_oct, mc, b, S5_CHUNK, LANES)
    y = jnp.transpose(y, (2, 1, 3, 0, 4))
    return y.reshape(b, mc * S5_CHUNK, n_oct * LANES)


def _attn_kernel(*refs, group, hd, chunks):
    qt_ref = refs[0]
    ot_ref = refs[-1]
    kv = refs[1:-1]
    tq = qt_ref.shape[2]
    qt = jnp.concatenate([qt_ref[0, h * hd:(h + 1) * hd, :] for h in range(group)], axis=1)
    cols = group * tq
    m = jnp.full((1, cols), -jnp.inf, F32)
    acc = jnp.zeros((hd + BF16_ROWS, cols), F32)
    for i, tk in enumerate(chunks):
        k_ref, vt_ref = kv[2 * i], kv[2 * i + 1]
        ones = jnp.ones((BF16_ROWS, tk), BF16)
        for c in range(k_ref.shape[1] // tk):
            k = k_ref[0, c * tk:(c + 1) * tk, :]
            vt = jnp.concatenate([vt_ref[0, :, c * tk:(c + 1) * tk], ones], axis=0)
            st = _dot(k, qt)
            m_new = jnp.maximum(m, jnp.max(st, axis=0, keepdims=True))
            alpha = jnp.exp2(m - m_new)
            p = jnp.exp2(st - m_new)
            acc = alpha * acc + _dot(vt, p.astype(BF16))
            m = m_new
    ot = acc[:hd] / acc[hd:hd + 1]
    for h in range(group):
        ot_ref[0, h * hd:(h + 1) * hd, :] = ot[:, h * tq:(h + 1) * tq].astype(BF16)


def _attention(qt, kvs, hd, tq):
    b, aw, t = qt.shape
    group = aw // hd // N_KV_HEADS
    gw = group * hd
    chunks = tuple(min(k.shape[1], 512) for k, _ in kvs)
    in_specs = [pl.BlockSpec((1, gw, tq), lambda bi, j, i: (bi, j, i))]
    args = [qt]
    for k, vt in kvs:
        s = k.shape[1]
        in_specs.append(pl.BlockSpec((1, s, hd), lambda bi, j, i: (bi, 0, j)))
        in_specs.append(pl.BlockSpec((1, hd, s), lambda bi, j, i: (bi, j, 0)))
        args += [k, vt]
    return pl.pallas_call(
        functools.partial(_attn_kernel, group=group, hd=hd, chunks=chunks),
        grid=(b, N_KV_HEADS, t // tq),
        in_specs=in_specs,
        out_specs=pl.BlockSpec((1, gw, tq), lambda bi, j, i: (bi, j, i)),
        out_shape=jax.ShapeDtypeStruct((b, aw, t), BF16),
        compiler_params=_params(("arbitrary", "arbitrary", "arbitrary")),
        name="attention",
    )(*args)


def _outproj_kernel(zc_ref, zp_ref, zn_ref, y_ref, at_ref, x_ref, mod_ref,
                    cwt_ref, wglu_ref, bglu_ref, wo_ref, g_ref, o_ref, ubuf_ref, *, cw):
    i = pl.program_id(1)
    tm = x_ref.shape[1]

    def gated(z):
        z = z.astype(F32)
        return z[:, 2 * cw:] * z[:, :cw]

    zc = zc_ref[0].astype(F32)
    u = zc[:, 2 * cw:] * zc[:, :cw]
    prev = gated(zp_ref[0])[BF16_ROWS - 1:BF16_ROWS]
    nxt = gated(zn_ref[0])[0:1]
    ubuf_ref[SUBLANES - 1:SUBLANES, :] = jnp.where(i > 0, prev, 0.0)
    ubuf_ref[SUBLANES:SUBLANES + tm, :] = u
    ubuf_ref[SUBLANES + tm:SUBLANES + tm + 1, :] = jnp.where(i < pl.num_programs(1) - 1, nxt, 0.0)
    y = (cwt_ref[0, 0:1] * ubuf_ref[SUBLANES - 1:SUBLANES - 1 + tm, :] + cwt_ref[0, 1:2] * u
         + cwt_ref[0, 2:3] * ubuf_ref[SUBLANES + 1:SUBLANES + 1 + tm, :])
    conv = (zc[:, cw:2 * cw] * y).astype(BF16)

    gl = jax.nn.gelu(y_ref[0].astype(F32))
    gate = jax.nn.sigmoid(_dot(gl.astype(BF16), wglu_ref[0]) + bglu_ref[0])
    ssm = (gl * gate).astype(BF16)

    ns = conv.shape[1] + ssm.shape[1]
    tn = (((0,), (0,)), ((), ()))
    mix = (_dot(jnp.concatenate([conv, ssm], axis=1), wo_ref[0, 0:ns, :])
           + lax.dot_general(at_ref[0], wo_ref[0, ns:, :], tn, preferred_element_type=F32))
    o_ref[0] = x_ref[0] + mod_ref[0, 2:3, :] * _rms(mix, g_ref[0])


def _outproj(zc, y, at, x, mod, conv_w, w_glu, b_glu, w_out, gain, layer, dims, tm):
    b, t, d = x.shape
    cw, sw, aw, hd = dims
    nhalo = tm // BF16_ROWS
    last = t // BF16_ROWS - 1
    tok = lambda w: pl.BlockSpec((1, tm, w), lambda bi, i: (bi, i, 0))
    return pl.pallas_call(
        functools.partial(_outproj_kernel, cw=cw),
        grid=(b, t // tm),
        in_specs=[
            tok(3 * cw),
            pl.BlockSpec((1, BF16_ROWS, 3 * cw), lambda bi, i: (bi, jnp.maximum(i * nhalo - 1, 0), 0)),
            pl.BlockSpec((1, BF16_ROWS, 3 * cw), lambda bi, i: (bi, jnp.minimum((i + 1) * nhalo, last), 0)),
            tok(sw),
            pl.BlockSpec((1, aw, tm), lambda bi, i: (bi, 0, i)),
            tok(d),
            pl.BlockSpec((1, MOD_ROWS, d), lambda bi, i: (bi, 0, 0)),
            _layer_spec((conv_w.shape[1], cw), layer),
            _layer_spec((sw, sw), layer),
            _layer_spec((1, sw), layer),
            _layer_spec((d, d), layer),
            _layer_spec((1, d), layer),
        ],
        out_specs=tok(d),
        out_shape=jax.ShapeDtypeStruct((b, t, d), F32),
        scratch_shapes=[pltpu.VMEM((tm + 2 * SUBLANES, cw), F32)],
        compiler_params=_params(("arbitrary", "arbitrary")),
        name="outproj",
    )(zc, zc, zc, y, at, x, mod, conv_w, w_glu, b_glu, w_out, gain)


def _ffn_kernel(x_ref, mod_ref, gpre_ref, gpost_ref, wg_ref, wu_ref, wd_ref, o_ref, h_ref, acc_ref, *, nj):
    j = pl.program_id(2)

    def hidden_tile(h):
        a = jax.nn.silu(_dot(h, wg_ref[0])) * _dot(h, wu_ref[0])
        return _dot(a.astype(BF16), wd_ref[0])

    def pre_norm():
        gain = gpre_ref[0] * (1.0 + mod_ref[0, 4:5, :])
        h = (_rms(x_ref[0], gain) + mod_ref[0, 3:4, :]).astype(BF16)
        h_ref[...] = h
        return h

    def post_norm(acc):
        o_ref[0] = x_ref[0] + mod_ref[0, 5:6, :] * _rms(acc, gpost_ref[0])

    if nj == 1:
        post_norm(hidden_tile(pre_norm()))
        return

    @pl.when(j == 0)
    def _():
        acc_ref[...] = hidden_tile(pre_norm())

    @pl.when((j > 0) & (j < nj - 1))
    def _():
        acc_ref[...] += hidden_tile(h_ref[...])

    @pl.when(j == nj - 1)
    def _():
        post_norm(acc_ref[...] + hidden_tile(h_ref[...]))


def _ffn(x, mod, g_pre, g_post, w_gate, w_up, w_down, layer, tm, th):
    b, t, d = x.shape
    f = w_gate.shape[-1]
    tok = pl.BlockSpec((1, tm, d), lambda bi, i, j: (bi, i, 0))
    return pl.pallas_call(
        functools.partial(_ffn_kernel, nj=f // th),
        grid=(b, t // tm, f // th),
        in_specs=[
            tok,
            pl.BlockSpec((1, MOD_ROWS, d), lambda bi, i, j: (bi, 0, 0)),
            _layer_spec((1, d), layer),
            _layer_spec((1, d), layer),
            pl.BlockSpec((1, d, th), lambda bi, i, j: (layer, 0, j)),
            pl.BlockSpec((1, d, th), lambda bi, i, j: (layer, 0, j)),
            pl.BlockSpec((1, th, d), lambda bi, i, j: (layer, j, 0)),
        ],
        out_specs=tok,
        out_shape=jax.ShapeDtypeStruct((b, t, d), F32),
        scratch_shapes=[pltpu.VMEM((tm, d), BF16), pltpu.VMEM((tm, d), F32)],
        input_output_aliases={0: 0},
        compiler_params=_params(("arbitrary", "arbitrary", "arbitrary")),
        name="ffn",
    )(x, mod, g_pre, g_post, w_gate, w_up, w_down)


def _rope_tables(t, hd):
    rows = t // GRID_W
    row = jnp.broadcast_to(jnp.arange(rows)[:, None], (rows, GRID_W)).reshape(-1).astype(F32)
    col = jnp.broadcast_to(jnp.arange(GRID_W)[None, :], (rows, GRID_W)).reshape(-1).astype(F32)
    half = hd // 2
    inv_freq = ROPE_THETA ** (-jnp.arange(0, half, 2, dtype=F32) / half)
    ar = row[:, None] * inv_freq
    ac = col[:, None] * inv_freq
    cos = jnp.concatenate([jnp.cos(ar), jnp.cos(ar), jnp.cos(ac), jnp.cos(ac)], axis=-1)
    sin = jnp.concatenate([-jnp.sin(ar), jnp.sin(ar), -jnp.sin(ac), jnp.sin(ac)], axis=-1)
    return cos, sin


def _tile(n, pref):
    return pref if n % pref == 0 else n


def kernel(x, c, ctx, c_ctx, w_mod, b_mod, g_pre_mix, g_post_mix, g_pre_ffn, g_post_ffn, w_in, conv_w, ssm_lam_re, ssm_lam_im, ssm_log_dt, ssm_b_re, ssm_b_im, ssm_c_re, ssm_c_im, ssm_d, w_glu, b_glu, q_norm, k_norm, w_out, w_gate, w_up, w_down):
    bsz, t, d = x.shape
    n_ctx = ctx.shape[1]
    depth = w_mod.shape[0]
    cw = conv_w.shape[-1]
    sw = w_glu.shape[-1]
    hd = q_norm.shape[-1]
    aw = d - cw - sw
    dims = (cw, sw, aw, hd)
    assert w_in.shape[-1] == 3 * cw + sw + aw + 2 * N_KV_HEADS * hd
    assert sw % LANES == 0 and t % (S5_CHUNK * BF16_ROWS) == 0 and n_ctx % (S5_CHUNK * BF16_ROWS) == 0

    cvec = jnp.concatenate([c, c_ctx[None, :], jnp.zeros((BF16_ROWS - bsz - 1, d), F32)], axis=0)
    mod = _modulation(cvec, w_mod, b_mod).reshape(depth, BF16_ROWS, N_MOD, d)
    mod = jnp.pad(mod, ((0, 0), (0, 0), (0, MOD_ROWS - N_MOD), (0, 0)))
    mod_x = mod[:, :bsz]
    mod_c = jnp.broadcast_to(mod[:, bsz:bsz + 1], (depth, bsz, MOD_ROWS, d))

    cos, sin = _rope_tables(t, hd)
    tabs = (cos, sin, cos.T, sin.T)
    ops = _s5_operators(ssm_lam_re, ssm_lam_im, ssm_log_dt, ssm_b_re, ssm_b_im, ssm_c_re, ssm_c_im, ssm_d)
    n_oct = sw // LANES

    r3 = lambda a: a.reshape(depth, 1, a.shape[-1])
    g_pre_mix, g_post_mix, g_pre_ffn, g_post_ffn = map(r3, (g_pre_mix, g_post_mix, g_pre_ffn, g_post_ffn))
    q_col = q_norm.reshape(depth, hd, 1)
    k_norm, b_glu = map(r3, (k_norm, b_glu))
    q_off = 3 * cw + sw
    k_off = q_off + aw
    v_off = k_off + N_KV_HEADS * hd
    w_a = w_in.astype(BF16)
    assert q_off % LANES == 0 and v_off % LANES == 0 and aw % LANES == 0
    qv_blocks = ([q_off // LANES + i for i in range(aw // LANES)]
                 + [v_off // LANES + i for i in range(N_KV_HEADS * hd // LANES)])
    w_bt = _transposed_columns(w_in, qv_blocks, LANES)
    w_glu, w_out, w_gate, w_up, w_down = (w.astype(BF16) for w in (w_glu, w_out, w_gate, w_up, w_down))

    tm_x, tm_c = _tile(t, 512), _tile(n_ctx, 256)
    tq_x, tq_c = _tile(t, 1024), _tile(n_ctx, 256)
    f = w_gate.shape[-1]
    th = max(c for c in range(LANES, 512 + 1, LANES) if f % c == 0)
    cb_x = _tile(t // S5_CHUNK, 128)
    cb_c = n_ctx // S5_CHUNK
    h_zero = jnp.zeros((2, n_oct, bsz, 2 * OCTET * ssm_lam_re.shape[-1]), F32)

    def s5(zq, h0, layer, cb):
        y, hfin = _s5(_batch_minor_rows(zq), ops, h0, layer, bsz, cb)
        return _from_chunk_layout(y, bsz), hfin

    xc = ctx
    for l in range(depth):
        want_ctx = l < depth - 1
        zc_x, zs_x, q_x, k_x, v_x = _inproj(x, mod_x[l], g_pre_mix, w_a, w_bt, q_col, k_norm, tabs, l, dims, True, tm_x)
        zc_c, zs_c, q_c, k_c, v_c = _inproj(xc, mod_c[l], g_pre_mix, w_a, w_bt, q_col, k_norm, tabs, l, dims, False, tm_c)
        y_c, h_ctx = s5(zs_c, h_zero, l, cb_c)
        y_x, _ = s5(zs_x, h_ctx, l, cb_x)
        at_x = _attention(q_x, [(k_x, v_x), (k_c, v_c)], hd, tq_x)
        x = _outproj(zc_x, y_x, at_x, x, mod_x[l], conv_w, w_glu, b_glu, w_out,
                     g_post_mix, l, dims, tm_x)
        x = _ffn(x, mod_x[l], g_pre_ffn, g_post_ffn, w_gate, w_up, w_down, l, tm_x, th)
        if want_ctx:
            at_c = _attention(q_c, [(k_c, v_c)], hd, tq_c)
            xc = _outproj(zc_c, y_c, at_c, xc, mod_c[l], conv_w, w_glu, b_glu, w_out,
                          g_post_mix, l, dims, tm_c)
            xc = _ffn(xc, mod_c[l], g_pre_ffn, g_post_ffn, w_gate, w_up, w_down, l, tm_c, th)
    return x
```

```python
import functools
import math

import jax
import jax.numpy as jnp
from jax import lax
from jax.experimental import pallas as pl
from jax.experimental.pallas import tpu as pltpu

F32 = jnp.float32
BF16 = jnp.bfloat16

GRID_W = 64
N_KV_HEADS = 2
N_MOD = 6
ROPE_THETA = 10000.0
RMS_EPS = 1e-6

LANES = 128
SUBLANES = 8
BF16_ROWS = 16
V7X_VMEM_LIMIT_BYTES = 56 * 1024 * 1024

S5_GROUP = 16
S5_CHUNK = LANES // S5_GROUP
OCTET = LANES // S5_GROUP

MOD_ROWS = SUBLANES


def _dot(a, b):
    return jnp.dot(a, b, preferred_element_type=F32)


def _rms(x, gain):
    ms = jnp.mean(x * x, axis=-1, keepdims=True)
    return x * lax.rsqrt(ms + RMS_EPS) * gain


def _layer_spec(shape, layer):
    nd = len(shape)
    return pl.BlockSpec((1,) + shape, lambda *_: (layer,) + (0,) * nd, pipeline_mode=pl.Buffered(1))


def _params(semantics):
    return pltpu.CompilerParams(dimension_semantics=semantics, vmem_limit_bytes=V7X_VMEM_LIMIT_BYTES)


def _mod_kernel(c_ref, w_ref, b_ref, o_ref):
    s = jax.nn.silu(c_ref[...]).astype(BF16)
    o_ref[0] = _dot(s, w_ref[0].astype(BF16)) + b_ref[0]


def _modulation(cvec, w_mod, b_mod):
    depth, d, n = w_mod.shape
    r = cvec.shape[0]
    tn = min(n, 1024)
    return pl.pallas_call(
        _mod_kernel,
        grid=(depth, n // tn),
        in_specs=[
            pl.BlockSpec((r, d), lambda l, j: (0, 0)),
            pl.BlockSpec((1, d, tn), lambda l, j: (l, 0, j)),
            pl.BlockSpec((1, 1, tn), lambda l, j: (l, 0, j)),
        ],
        out_specs=pl.BlockSpec((1, r, tn), lambda l, j: (l, 0, j)),
        out_shape=jax.ShapeDtypeStruct((depth, r, n), F32),
        compiler_params=_params(("arbitrary", "arbitrary")),
        name="modulation",
    )(cvec, w_mod, b_mod.reshape(depth, 1, n))


def _wt_kernel(blk_ref, w_ref, o_ref):
    del blk_ref
    o_ref[0] = w_ref[0].T.astype(BF16)


def _transposed_columns(w, col_blocks, tn):
    depth, d, _ = w.shape
    blocks = jnp.asarray(col_blocks, jnp.int32)
    return pl.pallas_call(
        _wt_kernel,
        grid_spec=pltpu.PrefetchScalarGridSpec(
            num_scalar_prefetch=1,
            grid=(depth, len(col_blocks)),
            in_specs=[pl.BlockSpec((1, d, tn), lambda l, j, blk: (l, 0, blk[j]))],
            out_specs=pl.BlockSpec((1, tn, d), lambda l, j, blk: (l, j, 0)),
        ),
        out_shape=jax.ShapeDtypeStruct((depth, len(col_blocks) * tn, d), BF16),
        compiler_params=_params(("arbitrary", "arbitrary")),
        name="transpose_weights",
    )(blocks, w)


def _swap32(y, lane_lo):
    return jnp.where(lane_lo, pltpu.roll(y, LANES - 32, 1), pltpu.roll(y, 32, 1))


def _swap32_rows(y):
    return jnp.concatenate([y[32:64], y[0:32], y[96:128], y[64:96]], axis=0)


def _inproj_kernel(x_ref, mod_ref, g_ref, wa_ref, wbt_ref, qg_ref, kg_ref, cos_ref, sin_ref, cost_ref, sint_ref,
                   zc_ref, zq_ref, qt_ref, k_ref, vt_ref, zs_scr, *, cw, sw, aw, hd, rope):
    x = x_ref[0]
    h = _rms(x, g_ref[0] * (1.0 + mod_ref[0, 1:2, :])) + mod_ref[0, 0:1, :]
    h = h.astype(BF16)
    kvw = N_KV_HEADS * hd
    k_off = 3 * cw + sw
    zc_ref[0] = _dot(h, wa_ref[0, :, 0:3 * cw]).astype(BF16)
    zs = _dot(h, wa_ref[0, :, 3 * cw:k_off])
    chunks = x.shape[0] // S5_CHUNK
    for o in range(sw // LANES):
        zs_scr[o] = zs[:, o * LANES:(o + 1) * LANES]
        for i in range(S5_CHUNK):
            piece = zs_scr[o, pl.ds(i, chunks, stride=S5_CHUNK), :]
            zq_ref[o, 0, :, i * LANES:(i + 1) * LANES] = piece.astype(BF16)

    tm = x.shape[0]
    nt = (((1,), (1,)), ((), ()))
    vt_ref[0] = lax.dot_general(wbt_ref[0, aw:aw + kvw, :], h, nt, preferred_element_type=F32).astype(BF16)

    lane_lo = (lax.broadcasted_iota(jnp.int32, (tm, hd), 1) & 32) == 0
    k = _dot(h, wa_ref[0, :, k_off + aw:k_off + aw + kvw])
    for i in range(N_KV_HEADS):
        y = _rms(k[:, i * hd:(i + 1) * hd], kg_ref[0])
        if rope:
            y = y * cos_ref[...] + _swap32(y, lane_lo) * sin_ref[...]
        k_ref[0, :, i * hd:(i + 1) * hd] = y.astype(BF16)

    qscale = hd ** -0.5 * math.log2(math.e)
    qt = lax.dot_general(wbt_ref[0, 0:aw, :], h, nt, preferred_element_type=F32)
    for i in range(aw // hd):
        z = qt[i * hd:(i + 1) * hd, :]
        ms = jnp.mean(z * z, axis=0, keepdims=True)
        y = z * lax.rsqrt(ms + RMS_EPS) * qg_ref[0]
        if rope:
            y = y * cost_ref[...] + _swap32_rows(y) * sint_ref[...]
        qt_ref[0, i * hd:(i + 1) * hd, :] = (y * qscale).astype(BF16)


def _inproj(x, mod, gain, w_a, w_bt, q_gain_col, k_gain, tabs, layer, dims, rope, tm):
    b, t, d = x.shape
    cw, sw, aw, hd = dims
    kvw = N_KV_HEADS * hd
    n_oct = sw // LANES
    cos, sin, cos_t, sin_t = tabs
    tok = lambda w: pl.BlockSpec((1, tm, w), lambda bi, i: (bi, i, 0))
    tok_t = lambda w: pl.BlockSpec((1, w, tm), lambda bi, i: (bi, 0, i))
    tab = pl.BlockSpec((tm, hd), lambda bi, i: (i, 0))
    tab_t = pl.BlockSpec((hd, tm), lambda bi, i: (0, i))
    sds = jax.ShapeDtypeStruct
    return pl.pallas_call(
        functools.partial(_inproj_kernel, cw=cw, sw=sw, aw=aw, hd=hd, rope=rope),
        grid=(b, t // tm),
        in_specs=[
            tok(d),
            pl.BlockSpec((1, MOD_ROWS, d), lambda bi, i: (bi, 0, 0)),
            _layer_spec((1, d), layer),
            _layer_spec((d, w_a.shape[-1]), layer),
            _layer_spec((aw + kvw, d), layer),
            _layer_spec((hd, 1), layer),
            _layer_spec((1, hd), layer),
            tab, tab, tab_t, tab_t,
        ],
        out_specs=[tok(3 * cw),
                   pl.BlockSpec((n_oct, 1, tm // S5_CHUNK, S5_CHUNK * LANES), lambda bi, i: (0, bi, i, 0)),
                   tok_t(aw), tok(kvw), tok_t(kvw)],
        out_shape=[sds((b, t, 3 * cw), BF16), sds((n_oct, b, t // S5_CHUNK, S5_CHUNK * LANES), BF16),
                   sds((b, aw, t), BF16), sds((b, t, kvw), BF16), sds((b, kvw, t), BF16)],
        scratch_shapes=[pltpu.VMEM((n_oct, tm, LANES), F32)],
        compiler_params=_params(("arbitrary", "arbitrary")),
        name="inproj",
    )(x, mod, gain, w_a, w_bt, q_gain_col, k_gain, cos, sin, cos_t, sin_t)


def _s5_kernel(x_ref, atc_ref, bfc_ref, bbc_ref, cfc_ref, cbc_ref, ra_ref, rb_ref, lam_ref, dvec_ref, h0_ref,
               y_ref, hfin_ref,
               out_ref, bf_ref, bb_ref, s_ref, hb_all_ref, h_ref,
               *, cb, nb, nblk, n_shift, p_shift):
    s = pl.program_id(1)
    half = h_ref.shape[-1] // 2
    blk = cb * nb

    def expand(comp_ref, r_ref, row_shift, col_shift):
        full = _dot(comp_ref[0, 0], r_ref[...])
        rg = (lax.broadcasted_iota(jnp.int32, full.shape, 0) >> row_shift) & (OCTET - 1)
        cg = (lax.broadcasted_iota(jnp.int32, full.shape, 1) >> col_shift) & (OCTET - 1)
        return jnp.where(rg == cg, full, 0.0).astype(out_ref.dtype)

    @pl.when(s == 0)
    def _():
        kx, sw = bf_ref.shape
        out_ref[0:kx, :] = expand(atc_ref, ra_ref, n_shift, n_shift)
        out_ref[kx:kx + sw, :] = expand(cfc_ref, ra_ref, p_shift, n_shift)
        out_ref[kx + sw:kx + 2 * sw, :] = expand(cbc_ref, ra_ref, p_shift, n_shift)
        bf_ref[...] = expand(bfc_ref, rb_ref, n_shift, p_shift)
        bb_ref[...] = expand(bbc_ref, rb_ref, n_shift, p_shift)
        h_ref[...] = h0_ref[1, 0]

    def scan(lam, reverse):
        lr = jnp.broadcast_to(lam[:, :half], (nb, half))
        li = jnp.broadcast_to(lam[:, half:], (nb, half))

        def body(k, carry):
            hr, hi = carry
            c = (cb - 1 - k) if reverse else k
            r0 = pl.multiple_of(c * nb, nb)
            srow = s_ref[pl.ds(r0, nb), :]
            s_ref[pl.ds(r0, nb), :] = jnp.concatenate([hr, hi], axis=1)
            nhr = lr * hr - li * hi + srow[:, :half]
            nhi = lr * hi + li * hr + srow[:, half:]
            return nhr, nhi

        hr, hi = lax.fori_loop(0, cb, body, (h_ref[:, :half], h_ref[:, half:]), unroll=2)
        h_ref[...] = jnp.concatenate([hr, hi], axis=1)

    @pl.when(s < nblk)
    def _():
        s_ref[...] = _dot(x_ref[0], bb_ref[...])
        scan(lam_ref[1, 0, 0], True)
        r0 = pl.multiple_of((nblk - 1 - s) * blk, blk)
        hb_all_ref[pl.ds(r0, blk), :] = s_ref[...].astype(hb_all_ref.dtype)

    @pl.when(s == nblk - 1)
    def _():
        hfin_ref[1, 0] = h_ref[...]
        h_ref[...] = h0_ref[0, 0]

    @pl.when(s >= nblk)
    def _():
        x = x_ref[0]
        s_ref[...] = _dot(x, bf_ref[...])
        scan(lam_ref[0, 0, 0], False)
        r0 = pl.multiple_of((s - nblk) * blk, blk)
        lhs = jnp.concatenate([x, s_ref[...].astype(out_ref.dtype), hb_all_ref[pl.ds(r0, blk), :]], axis=1)
        y = _dot(lhs, out_ref[...])
        y_ref[0] = (y + dvec_ref[0, 0] * x.astype(F32)).astype(y_ref.dtype)

    @pl.when(s == 2 * nblk - 1)
    def _():
        hfin_ref[0, 0] = h_ref[...]


def _s5(xq, ops, h0, layer, nb, cb):
    at_c, bf_c, bb_c, cf_c, cb_c, r_a, r_b, lam8, dvec, n_shift, p_shift = ops
    n_oct, rows, kx = xq.shape
    sw = lam8.shape[-1]
    blk = cb * nb
    nblk = rows // blk
    xblk = lambda o, s: (o, jnp.where(s < nblk, nblk - 1 - s, s - nblk), 0)
    yblk = lambda o, s: (o, jnp.maximum(s - nblk, 0), 0)
    comp = lambda a: pl.BlockSpec((1, 1) + a.shape[2:], lambda o, s: (layer, o, 0, 0))
    whole = lambda a: pl.BlockSpec(a.shape, lambda o, s: (0, 0))
    dt = xq.dtype
    return pl.pallas_call(
        functools.partial(_s5_kernel, cb=cb, nb=nb, nblk=nblk, n_shift=n_shift, p_shift=p_shift),
        grid=(n_oct, 2 * nblk),
        in_specs=[
            pl.BlockSpec((1, blk, kx), xblk),
            comp(at_c), comp(bf_c), comp(bb_c), comp(cf_c), comp(cb_c), whole(r_a), whole(r_b),
            pl.BlockSpec((2, 1, 1, 1, sw), lambda o, s: (0, layer, o, 0, 0)),
            pl.BlockSpec((1, 1, 1, kx), lambda o, s: (layer, o, 0, 0)),
            pl.BlockSpec((2, 1, nb, sw), lambda o, s: (0, o, 0, 0)),
        ],
        out_specs=[
            pl.BlockSpec((1, blk, kx), yblk),
            pl.BlockSpec((2, 1, nb, sw), lambda o, s: (0, o, 0, 0)),
        ],
        out_shape=[
            jax.ShapeDtypeStruct((n_oct, rows, kx), dt),
            jax.ShapeDtypeStruct((2, n_oct, nb, sw), F32),
        ],
        scratch_shapes=[
            pltpu.VMEM((kx + 2 * sw, kx), dt), pltpu.VMEM((kx, sw), dt), pltpu.VMEM((kx, sw), dt),
            pltpu.VMEM((blk, sw), F32), pltpu.VMEM((rows, sw), dt), pltpu.VMEM((nb, sw), F32),
        ],
        compiler_params=_params(("arbitrary", "arbitrary")),
        name="s5_mixer",
    )(xq, at_c, bf_c, bb_c, cf_c, cb_c, r_a, r_b, lam8, dvec, h0)


def _cmul(a, b):
    return a[0] * b[0] - a[1] * b[1], a[0] * b[1] + a[1] * b[0]


def _s5_operators(lam_re, lam_im, log_dt, b_re, b_im, c_re, c_im, d_skip):
    hp = lax.Precision.HIGHEST
    depth, _, g, p = lam_re.shape
    n = b_re.shape[-1]
    n_oct = g // OCTET
    m = S5_CHUNK
    assert n == S5_GROUP and p & (p - 1) == 0 and 2 * p == LANES
    dt = jnp.exp(log_dt)[..., None]
    mag = jnp.exp(lam_re * dt)
    lam_bar = (mag * jnp.cos(lam_im * dt), mag * jnp.sin(lam_im * dt))
    den = lam_re * lam_re + lam_im * lam_im
    num = (lam_bar[0] - 1.0, lam_bar[1])
    ratio = ((num[0] * lam_re + num[1] * lam_im) / den, (num[1] * lam_re - num[0] * lam_im) / den)
    bbar = _cmul((ratio[0][..., None], ratio[1][..., None]), (b_re, b_im))
    pows = [(jnp.ones_like(lam_re), jnp.zeros_like(lam_re))]
    for _ in range(m):
        pows.append(_cmul(pows[-1], lam_bar))
    pw = (jnp.stack([q[0] for q in pows]), jnp.stack([q[1] for q in pows]))

    def rows_by_token(blocks):
        w = blocks[0].shape[-1]
        st = jnp.stack([b.reshape(depth, n_oct, OCTET, n, w) for b in blocks], axis=2)
        return st.reshape(depth, n_oct, m * OCTET * n, w)

    def cols_by_token(blocks):
        st = jnp.stack(blocks, axis=-2)
        return st.reshape(st.shape[:-2] + (m * n,))

    def compact_in(e, d, exps):
        return rows_by_token([jnp.swapaxes(e[k][:, d], -1, -2) for k in exps])

    def compact_out(f, d, exps):
        return cols_by_token([jnp.swapaxes(f[k][:, d], -1, -2) for k in exps]).reshape(
            depth, n_oct, OCTET * p, m * n)

    e = [_cmul((pw[0][k][..., None], pw[1][k][..., None]), bbar) for k in range(m + 1)]
    f = [_cmul((c_re, c_im), (pw[0][k][:, :, :, None, :], pw[1][k][:, :, :, None, :])) for k in range(m + 1)]
    er, ei = [q[0] for q in e], [q[1] for q in e]
    fr, fi = [q[0] for q in f], [-q[1] for q in f]
    fexp = [m - 1 - i for i in range(m)]
    bexp = list(range(m))
    fout = [j + 1 for j in range(m)]
    bout = [m - j for j in range(m)]
    b_f = jnp.concatenate([compact_in(er, 0, fexp), compact_in(ei, 0, fexp)], axis=-1)
    b_b = jnp.concatenate([compact_in(er, 1, bexp), compact_in(ei, 1, bexp)], axis=-1)
    c_f = jnp.concatenate([compact_out(fr, 0, fout), compact_out(fi, 0, fout)], axis=-2)
    c_b = jnp.concatenate([compact_out(fr, 1, bout), compact_out(fi, 1, bout)], axis=-2)

    kk = jnp.stack([
        jnp.einsum('ldgop,ldgpn->ldgon', f[k][0], bbar[0], precision=hp)
        - jnp.einsum('ldgop,ldgpn->ldgon', f[k][1], bbar[1], precision=hp)
        for k in range(m)])
    kt = jnp.swapaxes(kk, -1, -2)

    def toeplitz_block(i, j):
        if j > i:
            return kt[j - i, :, 0]
        if j < i:
            return kt[i - j, :, 1]
        return kt[0, :, 0] + kt[0, :, 1]

    a_tot = rows_by_token([cols_by_token([toeplitz_block(i, j) for j in range(m)]) for i in range(m)])

    col = jnp.arange(OCTET * LANES)
    row = jnp.arange(LANES)
    r_a = (row[:, None] // n == col[None, :] // (OCTET * n)) & (row[:, None] % n == col[None, :] % n)
    r_b = (row[:, None] // p == col[None, :] // (OCTET * p)) & (row[:, None] % p == col[None, :] % p)

    lam8 = jnp.concatenate([pw[0][m].reshape(depth, 2, n_oct, 1, OCTET * p),
                            pw[1][m].reshape(depth, 2, n_oct, 1, OCTET * p)], axis=-1)
    lam8 = jnp.moveaxis(lam8, 1, 0)
    dvec = jnp.tile(d_skip.reshape(depth, n_oct, 1, LANES), (1, 1, 1, m))
    cast = lambda a: a.astype(BF16)
    return (cast(a_tot), cast(b_f), cast(b_b), cast(c_f), cast(c_b), cast(r_a), cast(r_b), lam8, dvec,
            n.bit_length() - 1, p.bit_length() - 1)


def _batch_minor_rows(zq):
    n_oct, b, chunks, w = zq.shape
    return jnp.swapaxes(zq, 1, 2).reshape(n_oct, chunks * b, w)


def _from_chunk_layout(y, b):
    n_oct, rows, _ = y.shape
    mc = rows // b
    y = y.reshape(n_oct, mc, b, S5_CHUNK, LANES)
    y = jnp.transpose(y, (2, 1, 3, 0, 4))
    return y.reshape(b, mc * S5_CHUNK, n_oct * LANES)


def _attn_kernel(*refs, group, hd, chunks):
    qt_ref = refs[0]
    ot_ref = refs[-1]
    kv = refs[1:-1]
    tq = qt_ref.shape[2]
    qt = jnp.concatenate([qt_ref[0, h * hd:(h + 1) * hd, :] for h in range(group)], axis=1)
    cols = group * tq
    m = jnp.full((1, cols), -jnp.inf, F32)
    acc = jnp.zeros((hd + BF16_ROWS, cols), F32)
    for i, tk in enumerate(chunks):
        k_ref, vt_ref = kv[2 * i], kv[2 * i + 1]
        ones = jnp.ones((BF16_ROWS, tk), BF16)
        for c in range(k_ref.shape[1] // tk):
            k = k_ref[0, c * tk:(c + 1) * tk, :]
            vt = jnp.concatenate([vt_ref[0, :, c * tk:(c + 1) * tk], ones], axis=0)
            st = _dot(k, qt)
            m_new = jnp.maximum(m, jnp.max(st, axis=0, keepdims=True))
            alpha = jnp.exp2(m - m_new)
            p = jnp.exp2(st - m_new)
            acc = alpha * acc + _dot(vt, p.astype(BF16))
            m = m_new
    ot = acc[:hd] / acc[hd:hd + 1]
    for h in range(group):
        ot_ref[0, h * hd:(h + 1) * hd, :] = ot[:, h * tq:(h + 1) * tq].astype(BF16)


def _attention(qt, kvs, hd, tq):
    b, aw, t = qt.shape
    group = aw // hd // N_KV_HEADS
    gw = group * hd
    chunks = tuple(min(k.shape[1], 512) for k, _ in kvs)
    in_specs = [pl.BlockSpec((1, gw, tq), lambda bi, j, i: (bi, j, i))]
    args = [qt]
    for k, vt in kvs:
        s = k.shape[1]
        in_specs.append(pl.BlockSpec((1, s, hd), lambda bi, j, i: (bi, 0, j)))
        in_specs.append(pl.BlockSpec((1, hd, s), lambda bi, j, i: (bi, j, 0)))
        args += [k, vt]
    return pl.pallas_call(
        functools.partial(_attn_kernel, group=group, hd=hd, chunks=chunks),
        grid=(b, N_KV_HEADS, t // tq),
        in_specs=in_specs,
        out_specs=pl.BlockSpec((1, gw, tq), lambda bi, j, i: (bi, j, i)),
        out_shape=jax.ShapeDtypeStruct((b, aw, t), BF16),
        compiler_params=_params(("arbitrary", "arbitrary", "arbitrary")),
        name="attention",
    )(*args)


def _outproj_kernel(zc_ref, zp_ref, zn_ref, y_ref, at_ref, x_ref, mod_ref,
                    cwt_ref, wglu_ref, bglu_ref, wo_ref, g_ref, o_ref, ubuf_ref, *, cw):
    i = pl.program_id(1)
    tm = x_ref.shape[1]

    def gated(z):
        z = z.astype(F32)
        return z[:, 2 * cw:] * z[:, :cw]

    zc = zc_ref[0].astype(F32)
    u = zc[:, 2 * cw:] * zc[:, :cw]
    prev = gated(zp_ref[0])[BF16_ROWS - 1:BF16_ROWS]
    nxt = gated(zn_ref[0])[0:1]
    ubuf_ref[SUBLANES - 1:SUBLANES, :] = jnp.where(i > 0, prev, 0.0)
    ubuf_ref[SUBLANES:SUBLANES + tm, :] = u
    ubuf_ref[SUBLANES + tm:SUBLANES + tm + 1, :] = jnp.where(i < pl.num_programs(1) - 1, nxt, 0.0)
    y = (cwt_ref[0, 0:1] * ubuf_ref[SUBLANES - 1:SUBLANES - 1 + tm, :] + cwt_ref[0, 1:2] * u
         + cwt_ref[0, 2:3] * ubuf_ref[SUBLANES + 1:SUBLANES + 1 + tm, :])
    conv = (zc[:, cw:2 * cw] * y).astype(BF16)

    gl = jax.nn.gelu(y_ref[0].astype(F32))
    gate = jax.nn.sigmoid(_dot(gl.astype(BF16), wglu_ref[0]) + bglu_ref[0])
    ssm = (gl * gate).astype(BF16)

    ns = conv.shape[1] + ssm.shape[1]
    tn = (((0,), (0,)), ((), ()))
    mix = (_dot(jnp.concatenate([conv, ssm], axis=1), wo_ref[0, 0:ns, :])
           + lax.dot_general(at_ref[0], wo_ref[0, ns:, :], tn, preferred_element_type=F32))
    o_ref[0] = x_ref[0] + mod_ref[0, 2:3, :] * _rms(mix, g_ref[0])


def _outproj(zc, y, at, x, mod, conv_w, w_glu, b_glu, w_out, gain, layer, dims, tm):
    b, t, d = x.shape
    cw, sw, aw, hd = dims
    nhalo = tm // BF16_ROWS
    last = t // BF16_ROWS - 1
    tok = lambda w: pl.BlockSpec((1, tm, w), lambda bi, i: (bi, i, 0))
    return pl.pallas_call(
        functools.partial(_outproj_kernel, cw=cw),
        grid=(b, t // tm),
        in_specs=[
            tok(3 * cw),
            pl.BlockSpec((1, BF16_ROWS, 3 * cw), lambda bi, i: (bi, jnp.maximum(i * nhalo - 1, 0), 0)),
            pl.BlockSpec((1, BF16_ROWS, 3 * cw), lambda bi, i: (bi, jnp.minimum((i + 1) * nhalo, last), 0)),
            tok(sw),
            pl.BlockSpec((1, aw, tm), lambda bi, i: (bi, 0, i)),
            tok(d),
            pl.BlockSpec((1, MOD_ROWS, d), lambda bi, i: (bi, 0, 0)),
            _layer_spec((conv_w.shape[1], cw), layer),
            _layer_spec((sw, sw), layer),
            _layer_spec((1, sw), layer),
            _layer_spec((d, d), layer),
            _layer_spec((1, d), layer),
        ],
        out_specs=tok(d),
        out_shape=jax.ShapeDtypeStruct((b, t, d), F32),
        scratch_shapes=[pltpu.VMEM((tm + 2 * SUBLANES, cw), F32)],
        compiler_params=_params(("arbitrary", "arbitrary")),
        name="outproj",
    )(zc, zc, zc, y, at, x, mod, conv_w, w_glu, b_glu, w_out, gain)


def _ffn_kernel(x_ref, mod_ref, gpre_ref, gpost_ref, wg_ref, wu_ref, wd_ref, o_ref, h_ref, acc_ref, *, nj):
    j = pl.program_id(2)

    def hidden_tile(h):
        a = jax.nn.silu(_dot(h, wg_ref[0])) * _dot(h, wu_ref[0])
        return _dot(a.astype(BF16), wd_ref[0])

    def pre_norm():
        gain = gpre_ref[0] * (1.0 + mod_ref[0, 4:5, :])
        h = (_rms(x_ref[0], gain) + mod_ref[0, 3:4, :]).astype(BF16)
        h_ref[...] = h
        return h

    def post_norm(acc):
        o_ref[0] = x_ref[0] + mod_ref[0, 5:6, :] * _rms(acc, gpost_ref[0])

    if nj == 1:
        post_norm(hidden_tile(pre_norm()))
        return

    @pl.when(j == 0)
    def _():
        acc_ref[...] = hidden_tile(pre_norm())

    @pl.when((j > 0) & (j < nj - 1))
    def _():
        acc_ref[...] += hidden_tile(h_ref[...])

    @pl.when(j == nj - 1)
    def _():
        post_norm(acc_ref[...] + hidden_tile(h_ref[...]))


def _ffn(x, mod, g_pre, g_post, w_gate, w_up, w_down, layer, tm, th):
    b, t, d = x.shape
    f = w_gate.shape[-1]
    tok = pl.BlockSpec((1, tm, d), lambda bi, i, j: (bi, i, 0))
    return pl.pallas_call(
        functools.partial(_ffn_kernel, nj=f // th),
        grid=(b, t // tm, f // th),
        in_specs=[
            tok,
            pl.BlockSpec((1, MOD_ROWS, d), lambda bi, i, j: (bi, 0, 0)),
            _layer_spec((1, d), layer),
            _layer_spec((1, d), layer),
            pl.BlockSpec((1, d, th), lambda bi, i, j: (layer, 0, j)),
            pl.BlockSpec((1, d, th), lambda bi, i, j: (layer, 0, j)),
            pl.BlockSpec((1, th, d), lambda bi, i, j: (layer, j, 0)),
        ],
        out_specs=tok,
        out_shape=jax.ShapeDtypeStruct((b, t, d), F32),
        scratch_shapes=[pltpu.VMEM((tm, d), BF16), pltpu.VMEM((tm, d), F32)],
        input_output_aliases={0: 0},
        compiler_params=_params(("arbitrary", "arbitrary", "arbitrary")),
        name="ffn",
    )(x, mod, g_pre, g_post, w_gate, w_up, w_down)


def _rope_tables(t, hd):
    rows = t // GRID_W
    row = jnp.broadcast_to(jnp.arange(rows)[:, None], (rows, GRID_W)).reshape(-1).astype(F32)
    col = jnp.broadcast_to(jnp.arange(GRID_W)[None, :], (rows, GRID_W)).reshape(-1).astype(F32)
    half = hd // 2
    inv_freq = ROPE_THETA ** (-jnp.arange(0, half, 2, dtype=F32) / half)
    ar = row[:, None] * inv_freq
    ac = col[:, None] * inv_freq
    cos = jnp.concatenate([jnp.cos(ar), jnp.cos(ar), jnp.cos(ac), jnp.cos(ac)], axis=-1)
    sin = jnp.concatenate([-jnp.sin(ar), jnp.sin(ar), -jnp.sin(ac), jnp.sin(ac)], axis=-1)
    return cos, sin


def _tile(n, pref):
    return pref if n % pref == 0 else n


def kernel(x, c, ctx, c_ctx, w_mod, b_mod, g_pre_mix, g_post_mix, g_pre_ffn, g_post_ffn, w_in, conv_w, ssm_lam_re, ssm_lam_im, ssm_log_dt, ssm_b_re, ssm_b_im, ssm_c_re, ssm_c_im, ssm_d, w_glu, b_glu, q_norm, k_norm, w_out, w_gate, w_up, w_down):
    bsz, t, d = x.shape
    n_ctx = ctx.shape[1]
    depth = w_mod.shape[0]
    cw = conv_w.shape[-1]
    sw = w_glu.shape[-1]
    hd = q_norm.shape[-1]
    aw = d - cw - sw
    dims = (cw, sw, aw, hd)
    assert w_in.shape[-1] == 3 * cw + sw + aw + 2 * N_KV_HEADS * hd
    assert sw % LANES == 0 and t % (S5_CHUNK * BF16_ROWS) == 0 and n_ctx % (S5_CHUNK * BF16_ROWS) == 0

    cvec = jnp.concatenate([c, c_ctx[None, :], jnp.zeros((BF16_ROWS - bsz - 1, d), F32)], axis=0)
    mod = _modulation(cvec, w_mod, b_mod).reshape(depth, BF16_ROWS, N_MOD, d)
    mod = jnp.pad(mod, ((0, 0), (0, 0), (0, MOD_ROWS - N_MOD), (0, 0)))
    mod_x = mod[:, :bsz]
    mod_c = jnp.broadcast_to(mod[:, bsz:bsz + 1], (depth, bsz, MOD_ROWS, d))

    cos, sin = _rope_tables(t, hd)
    tabs = (cos, sin, cos.T, sin.T)
    ops = _s5_operators(ssm_lam_re, ssm_lam_im, ssm_log_dt, ssm_b_re, ssm_b_im, ssm_c_re, ssm_c_im, ssm_d)
    n_oct = sw // LANES

    r3 = lambda a: a.reshape(depth, 1, a.shape[-1])
    g_pre_mix, g_post_mix, g_pre_ffn, g_post_ffn = map(r3, (g_pre_mix, g_post_mix, g_pre_ffn, g_post_ffn))
    q_col = q_norm.reshape(depth, hd, 1)
    k_norm, b_glu = map(r3, (k_norm, b_glu))
    q_off = 3 * cw + sw
    k_off = q_off + aw
    v_off = k_off + N_KV_HEADS * hd
    w_a = w_in.astype(BF16)
    assert q_off % LANES == 0 and v_off % LANES == 0 and aw % LANES == 0
    qv_blocks = ([q_off // LANES + i for i in range(aw // LANES)]
                 + [v_off // LANES + i for i in range(N_KV_HEADS * hd // LANES)])
    w_bt = _transposed_columns(w_in, qv_blocks, LANES)
    w_glu, w_out, w_gate, w_up, w_down = (w.astype(BF16) for w in (w_glu, w_out, w_gate, w_up, w_down))

    tm_x, tm_c = _tile(t, 512), _tile(n_ctx, 256)
    tq_x, tq_c = _tile(t, 1024), _tile(n_ctx, 256)
    f = w_gate.shape[-1]
    th = max(c for c in range(LANES, 512 + 1, LANES) if f % c == 0)
    cb_x = _tile(t // S5_CHUNK, 128)
    cb_c = n_ctx // S5_CHUNK
    h_zero = jnp.zeros((2, n_oct, bsz, 2 * OCTET * ssm_lam_re.shape[-1]), F32)

    def s5(zq, h0, layer, cb):
        y, hfin = _s5(_batch_minor_rows(zq), ops, h0, layer, bsz, cb)
        return _from_chunk_layout(y, bsz), hfin

    xc = ctx
    for l in range(depth):
        want_ctx = l < depth - 1
        zc_x, zs_x, q_x, k_x, v_x = _inproj(x, mod_x[l], g_pre_mix, w_a, w_bt, q_col, k_norm, tabs, l, dims, True, tm_x)
        zc_c, zs_c, q_c, k_c, v_c = _inproj(xc, mod_c[l], g_pre_mix, w_a, w_bt, q_col, k_norm, tabs, l, dims, False, tm_c)
        y_c, h_ctx = s5(zs_c, h_zero, l, cb_c)
        y_x, _ = s5(zs_x, h_ctx, l, cb_x)
        at_x = _attention(q_x, [(k_x, v_x), (k_c, v_c)], hd, tq_x)
        x = _outproj(zc_x, y_x, at_x, x, mod_x[l], conv_w, w_glu, b_glu, w_out,
                     g_post_mix, l, dims, tm_x)
        x = _ffn(x, mod_x[l], g_pre_ffn, g_post_ffn, w_gate, w_up, w_down, l, tm_x, th)
        if want_ctx:
            at_c = _attention(q_c, [(k_c, v_c)], hd, tq_c)
            xc = _outproj(zc_c, y_c, at_c, xc, mod_c[l], conv_w, w_glu, b_glu, w_out,
                          g_post_mix, l, dims, tm_c)
            xc = _ffn(xc, mod_c[l], g_pre_ffn, g_post_ffn, w_gate, w_up, w_down, l, tm_c, th)
    return x
```
